```python
import jax, jax.numpy as jnp
from jax import lax
import numpy as np

D_MODEL = 2048
BATCH = 2
SEQ = 4096
DEPTH = 2

HEAD_DIM = 64
HEADS_PER_GROUP = 16
DILATED_GROUPS = ((128, 1), (512, 4), (2048, 16))
N_ATT_HEADS = HEADS_PER_GROUP * len(DILATED_GROUPS)
ATT_WIDTH = N_ATT_HEADS * HEAD_DIM
ROPE_THETA = 500000.0
ROT_DIM = HEAD_DIM // 4

SG_CHUNK = 128
SG_WIDTH = 2 * D_MODEL
SG_GROUPS = 16
SG_GROUP_DIM = SG_WIDTH // SG_GROUPS

N_EXPERTS = 32
TOP_K = 4
EXPERT_FF = D_MODEL
SWIGLU_LIMIT = 7.0
SWIGLU_ALPHA = 1.702
MOE_BLOCK = 128

N_MIXERS = 2
N_MODULATIONS = 6
N_ATT_LAYERS = (DEPTH + 1) // 2
N_SG_LAYERS = DEPTH // 2
DEEPNORM_ALPHA = (2 * DEPTH) ** 0.25
DEEPNORM_BETA = (8 * DEPTH) ** -0.25
LN_EPS = 1e-5

kernel_name = 'hybrid_dilated_sgmlp_moe_deepnorm'


def layer_norm(x, g, b):
    xf = x.astype(jnp.float32)
    mu = jnp.mean(xf, axis=-1, keepdims=True)
    var = jnp.mean(jnp.square(xf - mu), axis=-1, keepdims=True)
    return ((xf - mu) * lax.rsqrt(var + LN_EPS) * g + b).astype(x.dtype)


def rotary_tables(positions):
    inv = jnp.power(jnp.float32(ROPE_THETA), -jnp.arange(0, ROT_DIM, 2, dtype=jnp.float32) / ROT_DIM)
    ang = positions.astype(jnp.float32)[..., None] * inv
    return jnp.cos(ang)[:, :, None, :], jnp.sin(ang)[:, :, None, :]


def apply_partial_rotary(t, cos, sin):
    half = ROT_DIM // 2
    t1 = t[..., :half].astype(jnp.float32)
    t2 = t[..., half:ROT_DIM].astype(jnp.float32)
    rot = jnp.concatenate([t1 * cos - t2 * sin, t2 * cos + t1 * sin], axis=-1).astype(t.dtype)
    return jnp.concatenate([rot, t[..., ROT_DIM:]], axis=-1)


def dilated_window_attention(q, k, v, window, dilation):
    B, S, H, E = q.shape
    L = S // dilation
    blk = window // dilation
    nblk = -(-L // blk)
    Lp = nblk * blk

    def strided(t):
        return t.reshape(B, L, dilation, H, E).transpose(0, 2, 1, 3, 4)

    qs = jnp.pad(strided(q), ((0, 0), (0, 0), (0, Lp - L), (0, 0), (0, 0))).reshape(B, dilation, nblk, blk, H, E)

    def key_pairs(t):
        tp = jnp.pad(strided(t), ((0, 0), (0, 0), (blk, Lp - L), (0, 0), (0, 0)))
        tp = tp.reshape(B, dilation, nblk + 1, blk, H, E)
        return jnp.concatenate([tp[:, :, :-1], tp[:, :, 1:]], axis=3)

    ks, vs = key_pairs(k), key_pairs(v)
    s = jnp.einsum('bdnqhe,bdnkhe->bdnhqk', qs, ks, preferred_element_type=jnp.float32) * (E ** -0.5)
    q_loc = jnp.arange(blk)[:, None] + blk
    k_loc = jnp.arange(2 * blk)[None, :]
    rel = q_loc - k_loc
    key_idx = jnp.arange(nblk)[:, None] * blk + jnp.arange(2 * blk)[None, :] - blk
    valid = ((rel >= 0) & (rel <= blk))[None, :, :] & (key_idx >= 0)[:, None, :]
    s = jnp.where(valid[None, None, :, None], s, -jnp.inf)
    m = jnp.max(s, axis=-1, keepdims=True)
    p = jnp.exp(s - m)
    den = jnp.sum(p, axis=-1, keepdims=True)
    o = jnp.einsum('bdnhqk,bdnkhe->bdnqhe', p / den, vs.astype(jnp.float32))
    lse = (m + jnp.log(den))[..., 0]
    o = o.reshape(B, dilation, Lp, H, E)[:, :, :L].transpose(0, 2, 1, 3, 4).reshape(B, S, H, E)
    lse = lse.transpose(0, 1, 2, 4, 3).reshape(B, dilation, Lp, H)[:, :, :L].transpose(0, 2, 1, 3).reshape(B, S, H)
    return o, lse


def dilated_attention_mixer(h, cos, sin, w_qkv, w_o):
    B, S, _ = h.shape
    qkv = (h @ w_qkv).reshape(B, S, 3, N_ATT_HEADS, HEAD_DIM)
    q = apply_partial_rotary(qkv[:, :, 0], cos, sin)
    k = apply_partial_rotary(qkv[:, :, 1], cos, sin)
    v = qkv[:, :, 2]
    outs, lses = [], []
    for g, (window, dilation) in enumerate(DILATED_GROUPS):
        sl = slice(g * HEADS_PER_GROUP, (g + 1) * HEADS_PER_GROUP)
        o, l = dilated_window_attention(q[:, :, sl], k[:, :, sl], v[:, :, sl], window, dilation)
        outs.append(o)
        lses.append(l)
    weights = jax.nn.softmax(jnp.stack(lses, axis=0), axis=0)
    mixed = jnp.concatenate([outs[g] * weights[g][..., None] for g in range(len(DILATED_GROUPS))], axis=2)
    return mixed.reshape(B, S, ATT_WIDTH).astype(h.dtype) @ w_o


def spatial_gating_mixer(h, w_in, b_in, ln_g, ln_b, w_sp, b_sp, w_out):
    B, S, _ = h.shape
    z = jax.nn.gelu(h @ w_in + b_in, approximate=False)
    u, v = z[..., :SG_WIDTH], z[..., SG_WIDTH:]
    v = layer_norm(v, ln_g, ln_b).reshape(B, S // SG_CHUNK, SG_CHUNK, SG_GROUPS, SG_GROUP_DIM)
    causal = jnp.tril(jnp.ones((SG_CHUNK, SG_CHUNK), dtype=bool))
    w_causal = jnp.where(causal[None], w_sp, 0)
    mixed = jnp.einsum('gts,bnsgc->bntgc', w_causal, v) + b_sp.T[None, None, :, :, None]
    gated = u * mixed.reshape(B, S, SG_WIDTH)
    return gated @ w_out


def clamped_swiglu(gu):
    gate = jnp.minimum(gu[..., :EXPERT_FF], SWIGLU_LIMIT)
    up = jnp.clip(gu[..., EXPERT_FF:], -SWIGLU_LIMIT, SWIGLU_LIMIT)
    return (up + 1.0) * gate * jax.nn.sigmoid(SWIGLU_ALPHA * gate)


def moe_ffn(h, router_w, router_b, w_in, b_in, w_out, b_out):
    B, S, D = h.shape
    T = B * S
    xt = h.reshape(T, D)
    logits = (xt @ router_w + router_b).astype(jnp.float32)
    top_val, top_idx = lax.top_k(logits, TOP_K)
    probs = jax.nn.softmax(top_val, axis=-1)
    flat_e = top_idx.reshape(-1)
    flat_t = jnp.repeat(jnp.arange(T, dtype=jnp.int32), TOP_K)
    flat_w = probs.reshape(-1)
    onehot = jax.nn.one_hot(flat_e, N_EXPERTS, dtype=jnp.int32)
    counts = jnp.sum(onehot, axis=0)
    rank = jnp.take_along_axis(jnp.cumsum(onehot, axis=0) - onehot, flat_e[:, None], axis=1)[:, 0]
    padded = (counts + MOE_BLOCK - 1) // MOE_BLOCK * MOE_BLOCK
    ends = jnp.cumsum(padded)
    starts = ends - padded
    dest = starts[flat_e] + rank
    n_rows = T * TOP_K + N_EXPERTS * MOE_BLOCK
    n_blocks = n_rows // MOE_BLOCK
    row_tok = jnp.zeros((n_rows,), jnp.int32).at[dest].set(flat_t)
    row_w = jnp.zeros((n_rows,), jnp.float32).at[dest].set(flat_w)
    block_start = jnp.arange(n_blocks) * MOE_BLOCK
    block_e = jnp.minimum(jnp.sum(block_start[:, None] >= ends[None, :], axis=1), N_EXPERTS - 1)

    def expert_block(args):
        tok, e = args
        xb = xt[tok]
        hid = clamped_swiglu(xb @ w_in[e] + b_in[e])
        return hid @ w_out[e] + b_out[e]

    yb = lax.map(expert_block, (row_tok.reshape(n_blocks, MOE_BLOCK), block_e))
    out = jnp.zeros((T, D), jnp.float32).at[row_tok].add(yb.reshape(n_rows, D) * row_w[:, None])
    return out.reshape(B, S, D).astype(h.dtype)


def setup_inputs(seed: int = 0) -> dict:
    key = jax.random.key(seed)
    ks = jax.random.split(key, 24)
    f32 = jnp.float32
    nrm = lambda k, shape, scale: jax.random.normal(k, shape, f32) * scale
    x = nrm(ks[0], (BATCH, SEQ, D_MODEL), 1.0)
    c = nrm(ks[1], (BATCH, D_MODEL), 1.0)
    offset = jax.random.randint(ks[2], (BATCH, 1), 0, 4096, dtype=jnp.int32)
    positions = offset + jnp.arange(SEQ, dtype=jnp.int32)[None, :]
    cond_w = nrm(ks[3], (DEPTH, D_MODEL, N_MODULATIONS * D_MODEL), 0.5 * D_MODEL ** -0.5)
    cond_b = nrm(ks[4], (DEPTH, N_MODULATIONS * D_MODEL), 0.02)
    ln_g = 1.0 + nrm(ks[5], (DEPTH, 2, D_MODEL), 0.02)
    ln_b = nrm(ks[6], (DEPTH, 2, D_MODEL), 0.02)
    attn_w_qkv = nrm(ks[7], (N_ATT_LAYERS, D_MODEL, 3 * ATT_WIDTH), D_MODEL ** -0.5)
    attn_w_o = nrm(ks[8], (N_ATT_LAYERS, ATT_WIDTH, D_MODEL), DEEPNORM_BETA * ATT_WIDTH ** -0.5)
    sg_w_in = nrm(ks[9], (N_SG_LAYERS, D_MODEL, 2 * SG_WIDTH), D_MODEL ** -0.5)
    sg_b_in = nrm(ks[10], (N_SG_LAYERS, 2 * SG_WIDTH), 0.02)
    sg_ln_g = 1.0 + nrm(ks[11], (N_SG_LAYERS, SG_WIDTH), 0.02)
    sg_ln_b = nrm(ks[12], (N_SG_LAYERS, SG_WIDTH), 0.02)
    sg_w_spatial = nrm(ks[13], (N_SG_LAYERS, SG_GROUPS, SG_CHUNK, SG_CHUNK), SG_CHUNK ** -0.5)
    sg_b_spatial = 1.0 + nrm(ks[14], (N_SG_LAYERS, SG_GROUPS, SG_CHUNK), 0.02)
    sg_w_out = nrm(ks[15], (N_SG_LAYERS, SG_WIDTH, D_MODEL), DEEPNORM_BETA * SG_WIDTH ** -0.5)
    router_w = nrm(ks[16], (DEPTH, D_MODEL, N_EXPERTS), D_MODEL ** -0.5)
    router_b = nrm(ks[17], (DEPTH, N_EXPERTS), 0.01)
    expert_w_in = nrm(ks[18], (DEPTH, N_EXPERTS, D_MODEL, 2 * EXPERT_FF), D_MODEL ** -0.5)
    expert_b_in = nrm(ks[19], (DEPTH, N_EXPERTS, 2 * EXPERT_FF), 0.02)
    expert_w_out = nrm(ks[20], (DEPTH, N_EXPERTS, EXPERT_FF, D_MODEL), DEEPNORM_BETA * EXPERT_FF ** -0.5)
    expert_b_out = nrm(ks[21], (DEPTH, N_EXPERTS, D_MODEL), 0.02)
    return {'x': x, 'c': c, 'positions': positions, 'cond_w': cond_w, 'cond_b': cond_b,
            'ln_g': ln_g, 'ln_b': ln_b, 'attn_w_qkv': attn_w_qkv, 'attn_w_o': attn_w_o,
            'sg_w_in': sg_w_in, 'sg_b_in': sg_b_in, 'sg_ln_g': sg_ln_g, 'sg_ln_b': sg_ln_b,
            'sg_w_spatial': sg_w_spatial, 'sg_b_spatial': sg_b_spatial, 'sg_w_out': sg_w_out,
            'router_w': router_w, 'router_b': router_b, 'expert_w_in': expert_w_in,
            'expert_b_in': expert_b_in, 'expert_w_out': expert_w_out, 'expert_b_out': expert_b_out}


def reference(x, c, positions, cond_w, cond_b, ln_g, ln_b, attn_w_qkv, attn_w_o,
              sg_w_in, sg_b_in, sg_ln_g, sg_ln_b, sg_w_spatial, sg_b_spatial, sg_w_out,
              router_w, router_b, expert_w_in, expert_b_in, expert_w_out, expert_b_out):
    cos, sin = rotary_tables(positions)
    c_act = jax.nn.silu(c)
    for i in range(DEPTH):
        mod = (c_act @ cond_w[i] + cond_b[i])[:, None, :]
        shift_m, scale_m, gate_m, shift_f, scale_f, gate_f = jnp.split(mod, N_MODULATIONS, axis=-1)
        h = x * (1.0 + scale_m) + shift_m
        j = i // N_MIXERS
        if i % N_MIXERS == 0:
            y = dilated_attention_mixer(h, cos, sin, attn_w_qkv[j], attn_w_o[j])
        else:
            y = spatial_gating_mixer(h, sg_w_in[j], sg_b_in[j], sg_ln_g[j], sg_ln_b[j],
                                     sg_w_spatial[j], sg_b_spatial[j], sg_w_out[j])
        x = layer_norm(DEEPNORM_ALPHA * x + (1.0 + gate_m) * y, ln_g[i, 0], ln_b[i, 0])
        h = x * (1.0 + scale_f) + shift_f
        y = moe_ffn(h, router_w[i], router_b[i], expert_w_in[i], expert_b_in[i],
                    expert_w_out[i], expert_b_out[i])
        x = layer_norm(DEEPNORM_ALPHA * x + (1.0 + gate_f) * y, ln_g[i, 1], ln_b[i, 1])
    return x
```

```python
import functools

import jax
import jax.numpy as jnp
from jax import lax
from jax.experimental import pallas as pl
from jax.experimental.pallas import tpu as pltpu

F32 = jnp.float32
BF16 = jnp.bfloat16

D_MODEL = 2048
DEPTH = 2
HEAD_DIM = 64
HEADS_PER_GROUP = 16
DILATED_GROUPS = ((128, 1), (512, 4), (2048, 16))
GROUP_WIDTH = HEADS_PER_GROUP * HEAD_DIM
ATT_WIDTH = GROUP_WIDTH * len(DILATED_GROUPS)
ROPE_THETA = 500000.0
ROT_DIM = HEAD_DIM // 4
ATT_BLOCK = 128

SG_CHUNK = 128
SG_WIDTH = 2 * D_MODEL
SG_GROUPS = 16
SG_GROUP_DIM = SG_WIDTH // SG_GROUPS

N_EXPERTS = 32
TOP_K = 4
EXPERT_FF = D_MODEL
SWIGLU_LIMIT = 7.0
SWIGLU_ALPHA = 1.702

N_MODULATIONS = 6
DEEPNORM_ALPHA = (2 * DEPTH) ** 0.25
LN_EPS = 1e-5
NEG_BIG = -1e30

LANES = 128
VMEM_LIMIT_BYTES = 56 * 1024 * 1024

MOD_TN = 512
ROPE_TM = 1024
PROJ_TM = 1024
PROJ_TN = 1024
MERGE_TM = 512
LN_TM = 256
TOPK_TM = 1024
SGU_TM = 256
MOE_TM = 256
FFN1_TF = 512
FFN2_TN = 1024
GATHER_ROWS = 512
COMBINE_TM = 256


def _cparams(sem):
    return pltpu.CompilerParams(dimension_semantics=sem, vmem_limit_bytes=VMEM_LIMIT_BYTES)


def _layer_norm(z, g, b):
    mu = jnp.mean(z, axis=-1, keepdims=True)
    zc = z - mu
    var = jnp.mean(zc * zc, axis=-1, keepdims=True)
    return zc * lax.rsqrt(var + LN_EPS) * g + b


def _mod_kernel(ct_ref, w_ref, b_ref, o_ref):
    ct = ct_ref[...]
    ca = ct * jax.nn.sigmoid(ct)
    w = w_ref[...]
    for b in range(ct.shape[1]):
        o_ref[b:b + 1, :] = jnp.sum(ca[:, b:b + 1] * w, axis=0, keepdims=True) + b_ref[...]


def _modulation(c, cond_w, cond_b):
    nl, d, n = cond_w.shape
    bsz = c.shape[0]
    return pl.pallas_call(
        _mod_kernel,
        grid=(nl, n // MOD_TN),
        in_specs=[
            pl.BlockSpec((d, bsz), lambda l, j: (0, 0)),
            pl.BlockSpec((None, d, MOD_TN), lambda l, j: (l, 0, j)),
            pl.BlockSpec((None, 1, MOD_TN), lambda l, j: (l, 0, j)),
        ],
        out_specs=pl.BlockSpec((None, bsz, MOD_TN), lambda l, j: (l, 0, j)),
        out_shape=jax.ShapeDtypeStruct((nl, bsz, n), F32),
        compiler_params=_cparams(("parallel", "parallel")),
        name="modulation",
    )(c.T, cond_w, cond_b.reshape(nl, 1, n))


def _rope_kernel(pos_ref, inv_ref, c_ref, s1_ref, s2_ref):
    ang = pos_ref[...] * inv_ref[...]
    lane = lax.broadcasted_iota(jnp.int32, ang.shape, 1) & (HEAD_DIM - 1)
    cs = jnp.cos(ang)
    sn = jnp.sin(ang)
    half = ROT_DIM // 2
    c_ref[...] = jnp.where(lane < ROT_DIM, cs, 1.0)
    s1_ref[...] = jnp.where(lane < half, -sn, 0.0)
    s2_ref[...] = jnp.where((lane >= half) & (lane < ROT_DIM), sn, 0.0)


def _rope_tables(positions):
    t = positions.size
    pos = positions.astype(F32).reshape(t, 1)
    inv = jnp.power(jnp.float32(ROPE_THETA), -jnp.arange(0, ROT_DIM, 2, dtype=F32) / ROT_DIM)
    lane = jnp.arange(LANES) % HEAD_DIM
    inv_row = inv[lane % (ROT_DIM // 2)].reshape(1, LANES)
    spec = pl.BlockSpec((ROPE_TM, LANES), lambda i: (i, 0))
    shp = jax.ShapeDtypeStruct((t, LANES), F32)
    return pl.pallas_call(
        _rope_kernel,
        grid=(t // ROPE_TM,),
        in_specs=[pl.BlockSpec((ROPE_TM, 1), lambda i: (i, 0)), pl.BlockSpec((1, LANES), lambda i: (0, 0))],
        out_specs=[spec, spec, spec],
        out_shape=[shp, shp, shp],
        compiler_params=_cparams(("parallel",)),
        name="rope_tables",
    )(pos, inv_row)


def _modulate_into(h_scr, x_ref, mod_ref):
    shift = mod_ref[0:1, :]
    scale = mod_ref[1:2, :]
    h_scr[...] = (x_ref[...] * (1.0 + scale) + shift).astype(BF16)


def _qkv_kernel(x_ref, mod_ref, w_ref, c_ref, s1_ref, s2_ref, o_ref, h_scr, *, n_rope_tiles):
    j = pl.program_id(1)

    @pl.when(j == 0)
    def _():
        _modulate_into(h_scr, x_ref, mod_ref)

    acc = jnp.dot(h_scr[...], w_ref[...], preferred_element_type=F32)

    @pl.when(j < n_rope_tiles)
    def _():
        c = c_ref[...]
        s1 = s1_ref[...]
        s2 = s2_ref[...]
        for cb in range(acc.shape[1] // LANES):
            a = acc[:, cb * LANES:(cb + 1) * LANES]
            up = pltpu.roll(a, LANES - ROT_DIM // 2, 1)
            dn = pltpu.roll(a, ROT_DIM // 2, 1)
            o_ref[:, cb * LANES:(cb + 1) * LANES] = (a * c + up * s1 + dn * s2).astype(BF16)

    @pl.when(j >= n_rope_tiles)
    def _():
        o_ref[...] = acc.astype(BF16)


def _qkv_projection(x2, mod_l, w_bf, tabs, seq):
    t, d = x2.shape
    n = w_bf.shape[1]
    tiles_per_seq = seq // PROJ_TM
    tab_spec = pl.BlockSpec((PROJ_TM, LANES), lambda i, j: (i, 0))
    return pl.pallas_call(
        functools.partial(_qkv_kernel, n_rope_tiles=2 * ATT_WIDTH // PROJ_TN),
        grid=(t // PROJ_TM, n // PROJ_TN),
        in_specs=[
            pl.BlockSpec((PROJ_TM, d), lambda i, j: (i, 0)),
            pl.BlockSpec((None, N_MODULATIONS, d), lambda i, j: (i // tiles_per_seq, 0, 0)),
            pl.BlockSpec((d, PROJ_TN), lambda i, j: (0, j)),
            tab_spec, tab_spec, tab_spec,
        ],
        out_specs=pl.BlockSpec((PROJ_TM, PROJ_TN), lambda i, j: (i, j)),
        out_shape=jax.ShapeDtypeStruct((t, n), BF16),
        scratch_shapes=[pltpu.VMEM((PROJ_TM, d), BF16)],
        compiler_params=_cparams(("arbitrary", "arbitrary")),
        name="qkv_projection",
    )(x2, mod_l, w_bf, *tabs)


def _sg_in_kernel(x_ref, mod_ref, w_ref, b_ref, o_ref, h_scr):
    @pl.when(pl.program_id(1) == 0)
    def _():
        _modulate_into(h_scr, x_ref, mod_ref)

    z = jnp.dot(h_scr[...], w_ref[...], preferred_element_type=F32) + b_ref[...]
    o_ref[...] = (0.5 * z * (1.0 + lax.erf(z * (2.0 ** -0.5)))).astype(BF16)


def _sg_in_projection(x2, mod_l, w_bf, b_in, seq):
    t, d = x2.shape
    n = w_bf.shape[1]
    tiles_per_seq = seq // PROJ_TM
    return pl.pallas_call(
        _sg_in_kernel,
        grid=(t // PROJ_TM, n // PROJ_TN),
        in_specs=[
            pl.BlockSpec((PROJ_TM, d), lambda i, j: (i, 0)),
            pl.BlockSpec((None, N_MODULATIONS, d), lambda i, j: (i // tiles_per_seq, 0, 0)),
            pl.BlockSpec((d, PROJ_TN), lambda i, j: (0, j)),
            pl.BlockSpec((1, PROJ_TN), lambda i, j: (0, j)),
        ],
        out_specs=pl.BlockSpec((PROJ_TM, PROJ_TN), lambda i, j: (i, j)),
        out_shape=jax.ShapeDtypeStruct((t, n), BF16),
        scratch_shapes=[pltpu.VMEM((PROJ_TM, d), BF16)],
        compiler_params=_cparams(("arbitrary", "arbitrary")),
        name="sg_in_projection",
    )(x2, mod_l, w_bf, b_in.reshape(1, n))


def _attn_kernel(q_ref, kp_ref, kc_ref, vp_ref, vc_ref, o_ref, lse_ref):
    n = pl.program_id(2)
    blk = q_ref.shape[0]
    row = lax.broadcasted_iota(jnp.int32, (blk, blk), 0)
    col = lax.broadcasted_iota(jnp.int32, (blk, blk), 1)
    prev_ok = (col >= row) & (n > 0)
    cur_ok = col <= row
    scale = HEAD_DIM ** -0.5
    dn = (((1,), (1,)), ((), ()))
    for h in range(HEADS_PER_GROUP):
        sl = slice(h * HEAD_DIM, (h + 1) * HEAD_DIM)
        q = q_ref[:, sl]
        sp = lax.dot_general(q, kp_ref[:, sl], dn, preferred_element_type=F32) * scale
        sc = lax.dot_general(q, kc_ref[:, sl], dn, preferred_element_type=F32) * scale
        sp = jnp.where(prev_ok, sp, NEG_BIG)
        sc = jnp.where(cur_ok, sc, NEG_BIG)
        m = jnp.maximum(jnp.max(sp, axis=-1, keepdims=True), jnp.max(sc, axis=-1, keepdims=True))
        pp = jnp.exp(sp - m)
        pc = jnp.exp(sc - m)
        den = jnp.sum(pp, axis=-1, keepdims=True) + jnp.sum(pc, axis=-1, keepdims=True)
        o = jnp.dot(pp.astype(BF16), vp_ref[:, sl], preferred_element_type=F32)
        o = o + jnp.dot(pc.astype(BF16), vc_ref[:, sl], preferred_element_type=F32)
        o_ref[:, sl] = (o / den).astype(BF16)
        lse_ref[:, sl] = jnp.broadcast_to(m + jnp.log(den), (blk, HEAD_DIM))


def _dilated_attention(qkv, bsz, seq, group, dilation):
    length = seq // dilation
    nblk = length // ATT_BLOCK
    width = qkv.shape[1]
    per_row = width // GROUP_WIDTH
    n_groups = len(DILATED_GROUPS)
    view = qkv.reshape(bsz, length, dilation * width)

    def spec(part, prev):
        def index(b, r, n):
            return (b, jnp.maximum(n - 1, 0) if prev else n, r * per_row + part * n_groups + group)
        return pl.BlockSpec((None, ATT_BLOCK, GROUP_WIDTH), index)

    out_spec = pl.BlockSpec((None, ATT_BLOCK, GROUP_WIDTH), lambda b, r, n: (b, n, r))
    o, lse = pl.pallas_call(
        _attn_kernel,
        grid=(bsz, dilation, nblk),
        in_specs=[spec(0, False), spec(1, True), spec(1, False), spec(2, True), spec(2, False)],
        out_specs=[out_spec, out_spec],
        out_shape=[jax.ShapeDtypeStruct((bsz, length, dilation * GROUP_WIDTH), BF16),
                   jax.ShapeDtypeStruct((bsz, length, dilation * GROUP_WIDTH), F32)],
        compiler_params=_cparams(("parallel", "parallel", "parallel")),
        name=f"dilated_attention_d{dilation}",
    )(view, view, view, view, view)
    return o.reshape(bsz * seq, GROUP_WIDTH), lse.reshape(bsz * seq, GROUP_WIDTH)


def _merge_kernel(o0, o1, o2, l0, l1, l2, out_ref):
    ls = [l0[...], l1[...], l2[...]]
    mx = jnp.maximum(jnp.maximum(ls[0], ls[1]), ls[2])
    es = [jnp.exp(l - mx) for l in ls]
    inv = 1.0 / (es[0] + es[1] + es[2])
    for g, o in enumerate((o0, o1, o2)):
        out_ref[:, g * GROUP_WIDTH:(g + 1) * GROUP_WIDTH] = (o[...].astype(F32) * (es[g] * inv)).astype(BF16)


def _merge_groups(outs, lses):
    t = outs[0].shape[0]
    spec = pl.BlockSpec((MERGE_TM, GROUP_WIDTH), lambda i: (i, 0))
    return pl.pallas_call(
        _merge_kernel,
        grid=(t // MERGE_TM,),
        in_specs=[spec] * 6,
        out_specs=pl.BlockSpec((MERGE_TM, ATT_WIDTH), lambda i: (i, 0)),
        out_shape=jax.ShapeDtypeStruct((t, ATT_WIDTH), BF16),
        compiler_params=_cparams(("parallel",)),
        name="merge_groups",
    )(*outs, *lses)


def _sgu_kernel(u_ref, v_ref, g_ref, b_ref, wsp_ref, bsp_ref, o_ref):
    v = v_ref[...].astype(F32)
    vn = _layer_norm(v, g_ref[...], b_ref[...]).astype(BF16)
    row = lax.broadcasted_iota(jnp.int32, (SG_CHUNK, SG_CHUNK), 0)
    col = lax.broadcasted_iota(jnp.int32, (SG_CHUNK, SG_CHUNK), 1)
    causal = col <= row
    for g in range(SG_GROUPS):
        w = jnp.where(causal, wsp_ref[g], 0.0).astype(BF16)
        bias = bsp_ref[:, g:g + 1]
        cs = slice(g * SG_GROUP_DIM, (g + 1) * SG_GROUP_DIM)
        for c in range(u_ref.shape[0] // SG_CHUNK):
            rs = slice(c * SG_CHUNK, (c + 1) * SG_CHUNK)
            mixed = jnp.dot(w, vn[rs, cs], preferred_element_type=F32) + bias
            o_ref[rs, cs] = (u_ref[rs, cs].astype(F32) * mixed).astype(BF16)


def _spatial_gating(z, ln_g, ln_b, w_sp, b_sp):
    t = z.shape[0]
    return pl.pallas_call(
        _sgu_kernel,
        grid=(t // SGU_TM,),
        in_specs=[
            pl.BlockSpec((SGU_TM, SG_WIDTH), lambda i: (i, 0)),
            pl.BlockSpec((SGU_TM, SG_WIDTH), lambda i: (i, 1)),
            pl.BlockSpec((1, SG_WIDTH), lambda i: (0, 0)),
            pl.BlockSpec((1, SG_WIDTH), lambda i: (0, 0)),
            pl.BlockSpec((SG_GROUPS, SG_CHUNK, SG_CHUNK), lambda i: (0, 0, 0)),
            pl.BlockSpec((SG_CHUNK, SG_GROUPS), lambda i: (0, 0)),
        ],
        out_specs=pl.BlockSpec((SGU_TM, SG_WIDTH), lambda i: (i, 0)),
        out_shape=jax.ShapeDtypeStruct((t, SG_WIDTH), BF16),
        compiler_params=_cparams(("parallel",)),
        name="spatial_gating",
    )(z, z, ln_g.reshape(1, -1), ln_b.reshape(1, -1), w_sp, b_sp.T)


def _proj_ln_kernel(a_ref, w_ref, x_ref, mod_ref, g_ref, b_ref, rw_hi_ref, rw_lo_ref, rb_ref,
                    x_out, h_out, logit_out):
    y = jnp.dot(a_ref[...], w_ref[...], preferred_element_type=F32)
    gate = mod_ref[2:3, :]
    xn = _layer_norm(DEEPNORM_ALPHA * x_ref[...] + (1.0 + gate) * y, g_ref[...], b_ref[...])
    x_out[...] = xn
    h = xn * (1.0 + mod_ref[4:5, :]) + mod_ref[3:4, :]
    h_out[...] = h
    h_hi = h.astype(BF16)
    h_lo = (h - h_hi.astype(F32)).astype(BF16)
    rw_hi = rw_hi_ref[...]
    logits = jnp.dot(h_hi, rw_hi, preferred_element_type=F32)
    logits = logits + jnp.dot(h_hi, rw_lo_ref[...], preferred_element_type=F32)
    logits = logits + jnp.dot(h_lo, rw_hi, preferred_element_type=F32)
    logit_out[...] = logits + rb_ref[...]


def _proj_ln(a, w_bf, x2, mod_l, ln_g, ln_b, router_w, router_b, seq):
    t, d = x2.shape
    k = a.shape[1]
    tiles_per_seq = seq // LN_TM
    rw_hi = router_w.astype(BF16)
    rw_lo = (router_w - rw_hi.astype(F32)).astype(BF16)
    row = lambda i: (i, 0)
    fixed = lambda i: (0, 0)
    return pl.pallas_call(
        _proj_ln_kernel,
        grid=(t // LN_TM,),
        in_specs=[
            pl.BlockSpec((LN_TM, k), row),
            pl.BlockSpec((k, d), fixed, pipeline_mode=pl.Buffered(1)),
            pl.BlockSpec((LN_TM, d), row),
            pl.BlockSpec((None, N_MODULATIONS, d), lambda i: (i // tiles_per_seq, 0, 0)),
            pl.BlockSpec((1, d), fixed),
            pl.BlockSpec((1, d), fixed),
            pl.BlockSpec((d, N_EXPERTS), fixed),
            pl.BlockSpec((d, N_EXPERTS), fixed),
            pl.BlockSpec((1, N_EXPERTS), fixed),
        ],
        out_specs=[pl.BlockSpec((LN_TM, d), row), pl.BlockSpec((LN_TM, d), row),
                   pl.BlockSpec((LN_TM, N_EXPERTS), row)],
        out_shape=[jax.ShapeDtypeStruct((t, d), F32), jax.ShapeDtypeStruct((t, d), F32),
                   jax.ShapeDtypeStruct((t, N_EXPERTS), F32)],
        compiler_params=_cparams(("parallel",)),
        name="proj_ln",
    )(a, w_bf, x2, mod_l, ln_g.reshape(1, d), ln_b.reshape(1, d), rw_hi, rw_lo,
      router_b.reshape(1, N_EXPERTS))


def _topk_kernel(l_ref, idx_ref, p_ref):
    l = l_ref[...]
    tm = l.shape[0]
    lane = lax.broadcasted_iota(jnp.int32, l.shape, 1)
    out_lane = lax.broadcasted_iota(jnp.int32, (tm, TOP_K), 1)
    idx_out = jnp.zeros((tm, TOP_K), jnp.int32)
    val_out = jnp.zeros((tm, TOP_K), F32)
    for k in range(TOP_K):
        m = jnp.max(l, axis=-1, keepdims=True)
        idx = jnp.min(jnp.where(l == m, lane, N_EXPERTS), axis=-1, keepdims=True)
        idx_out = jnp.where(out_lane == k, idx, idx_out)
        val_out = jnp.where(out_lane == k, m, val_out)
        l = jnp.where(lane == idx, -jnp.inf, l)
    e = jnp.exp(val_out - jnp.max(val_out, axis=-1, keepdims=True))
    idx_ref[...] = idx_out
    p_ref[...] = e / jnp.sum(e, axis=-1, keepdims=True)


def _route_topk(logits):
    t = logits.shape[0]
    return pl.pallas_call(
        _topk_kernel,
        grid=(t // TOPK_TM,),
        in_specs=[pl.BlockSpec((TOPK_TM, N_EXPERTS), lambda i: (i, 0))],
        out_specs=[pl.BlockSpec((TOPK_TM, TOP_K), lambda i: (i, 0))] * 2,
        out_shape=[jax.ShapeDtypeStruct((t, TOP_K), jnp.int32), jax.ShapeDtypeStruct((t, TOP_K), F32)],
        compiler_params=_cparams(("parallel",)),
        name="route_topk",
    )(logits)


def _row_copy(src_hbm, dst_vmem, sem, src_row, dst_row):
    return pltpu.make_async_copy(src_hbm.at[pl.ds(src_row, 1)], dst_vmem.at[pl.ds(dst_row, 1)], sem)


def _gather_kernel(tok_ref, h_hbm, o_ref, buf, sem):
    base = pl.program_id(0) * GATHER_ROWS

    def start(r, carry):
        _row_copy(h_hbm, buf, sem, tok_ref[base + r], r).start()
        return carry

    def wait(r, carry):
        _row_copy(h_hbm, buf, sem, 0, r).wait()
        return carry

    lax.fori_loop(0, GATHER_ROWS, start, 0)
    lax.fori_loop(0, GATHER_ROWS, wait, 0)
    o_ref[...] = buf[...].astype(BF16)


def _moe_gather(h, row_tok):
    n_rows = row_tok.shape[0]
    d = h.shape[1]
    return pl.pallas_call(
        _gather_kernel,
        grid_spec=pltpu.PrefetchScalarGridSpec(
            num_scalar_prefetch=1,
            grid=(n_rows // GATHER_ROWS,),
            in_specs=[pl.BlockSpec(memory_space=pl.ANY)],
            out_specs=pl.BlockSpec((GATHER_ROWS, d), lambda i, tok: (i, 0)),
            scratch_shapes=[pltpu.VMEM((GATHER_ROWS, d), F32), pltpu.SemaphoreType.DMA(())],
        ),
        out_shape=jax.ShapeDtypeStruct((n_rows, d), BF16),
        compiler_params=_cparams(("arbitrary",)),
        name="moe_gather",
    )(row_tok, h)


def _tile_state(te_ref, nu_ref):
    i = pl.program_id(1)
    n_used = nu_ref[0]
    cur = te_ref[jnp.minimum(i, n_used - 1)]
    prev = te_ref[jnp.minimum(jnp.maximum(i - 1, 0), n_used - 1)]
    active = i < n_used
    fresh = active & ((i == 0) | (cur != prev))
    return active, fresh


def _ffn1_kernel(te_ref, nu_ref, x_ref, wg_ref, wu_ref, bg_ref, bu_ref, o_ref, wg_bf, wu_bf):
    active, fresh = _tile_state(te_ref, nu_ref)

    @pl.when(fresh)
    def _():
        wg_bf[...] = wg_ref[...].astype(BF16)
        wu_bf[...] = wu_ref[...].astype(BF16)

    @pl.when(active)
    def _():
        x = x_ref[...]
        g = jnp.dot(x, wg_bf[...], preferred_element_type=F32) + bg_ref[...]
        u = jnp.dot(x, wu_bf[...], preferred_element_type=F32) + bu_ref[...]
        gate = jnp.minimum(g, SWIGLU_LIMIT)
        up = jnp.clip(u, -SWIGLU_LIMIT, SWIGLU_LIMIT)
        o_ref[...] = ((up + 1.0) * gate * jax.nn.sigmoid(SWIGLU_ALPHA * gate)).astype(BF16)

    @pl.when(jnp.logical_not(active))
    def _():
        o_ref[...] = jnp.zeros_like(o_ref)


def _expert_ffn1(xs, tile_e, n_used, w_in, b_in, layer):
    n_rows, d = xs.shape
    _, ne, _, two_ff = w_in.shape
    ff = two_ff // 2
    nf = ff // FFN1_TF
    nt = n_rows // MOE_TM

    def tile(i, nu):
        return jnp.minimum(i, nu[0] - 1)

    return pl.pallas_call(
        _ffn1_kernel,
        grid_spec=pltpu.PrefetchScalarGridSpec(
            num_scalar_prefetch=2,
            grid=(nf, nt),
            in_specs=[
                pl.BlockSpec((MOE_TM, d), lambda f, i, te, nu: (tile(i, nu), 0)),
                pl.BlockSpec((None, None, d, FFN1_TF), lambda f, i, te, nu: (layer, te[tile(i, nu)], 0, f)),
                pl.BlockSpec((None, None, d, FFN1_TF), lambda f, i, te, nu: (layer, te[tile(i, nu)], 0, nf + f)),
                pl.BlockSpec((None, 1, FFN1_TF), lambda f, i, te, nu: (te[tile(i, nu)], 0, f)),
                pl.BlockSpec((None, 1, FFN1_TF), lambda f, i, te, nu: (te[tile(i, nu)], 0, nf + f)),
            ],
            out_specs=pl.BlockSpec((MOE_TM, FFN1_TF), lambda f, i, te, nu: (i, f)),
            scratch_shapes=[pltpu.VMEM((d, FFN1_TF), BF16), pltpu.VMEM((d, FFN1_TF), BF16)],
        ),
        out_shape=jax.ShapeDtypeStruct((n_rows, ff), BF16),
        compiler_params=_cparams(("arbitrary", "arbitrary")),
        name="expert_ffn1",
    )(tile_e, n_used, xs, w_in, w_in, b_in.reshape(ne, 1, two_ff), b_in.reshape(ne, 1, two_ff))


def _ffn2_kernel(te_ref, nu_ref, h_ref, w_ref, b_ref, o_ref, w_bf):
    active, fresh = _tile_state(te_ref, nu_ref)

    @pl.when(fresh)
    def _():
        w_bf[...] = w_ref[...].astype(BF16)

    @pl.when(active)
    def _():
        o_ref[...] = jnp.dot(h_ref[...], w_bf[...], preferred_element_type=F32) + b_ref[...]

    @pl.when(jnp.logical_not(active))
    def _():
        o_ref[...] = jnp.zeros_like(o_ref)


def _expert_ffn2(hid, tile_e, n_used, w_out, b_out, layer):
    n_rows, ff = hid.shape
    _, ne, _, d = w_out.shape
    nn = d // FFN2_TN
    nt = n_rows // MOE_TM

    def tile(i, nu):
        return jnp.minimum(i, nu[0] - 1)

    return pl.pallas_call(
        _ffn2_kernel,
        grid_spec=pltpu.PrefetchScalarGridSpec(
            num_scalar_prefetch=2,
            grid=(nn, nt),
            in_specs=[
                pl.BlockSpec((MOE_TM, ff), lambda n, i, te, nu: (tile(i, nu), 0)),
                pl.BlockSpec((None, None, ff, FFN2_TN), lambda n, i, te, nu: (layer, te[tile(i, nu)], 0, n)),
                pl.BlockSpec((None, 1, FFN2_TN), lambda n, i, te, nu: (te[tile(i, nu)], 0, n)),
            ],
            out_specs=pl.BlockSpec((MOE_TM, FFN2_TN), lambda n, i, te, nu: (i, n)),
            scratch_shapes=[pltpu.VMEM((ff, FFN2_TN), BF16)],
        ),
        out_shape=jax.ShapeDtypeStruct((n_rows, d), F32),
        compiler_params=_cparams(("arbitrary", "arbitrary")),
        name="expert_ffn2",
    )(tile_e, n_used, hid, w_out, b_out.reshape(ne, 1, d))


def _combine_kernel(dest_ref, y_hbm, p_ref, x_ref, mod_ref, g_ref, b_ref, o_ref, buf, sem):
    base = pl.program_id(0) * (COMBINE_TM * TOP_K)

    def start(r, carry):
        for k in range(TOP_K):
            _row_copy(y_hbm, buf.at[k], sem, dest_ref[base + r * TOP_K + k], r).start()
        return carry

    def wait(r, carry):
        for k in range(TOP_K):
            _row_copy(y_hbm, buf.at[k], sem, 0, r).wait()
        return carry

    lax.fori_loop(0, COMBINE_TM, start, 0)
    lax.fori_loop(0, COMBINE_TM, wait, 0)
    p = p_ref[...]
    moe = p[:, 0:1] * buf[0]
    for k in range(1, TOP_K):
        moe = moe + p[:, k:k + 1] * buf[k]
    gate = mod_ref[5:6, :]
    o_ref[...] = _layer_norm(DEEPNORM_ALPHA * x_ref[...] + (1.0 + gate) * moe, g_ref[...], b_ref[...])


def _combine_ln(y, dest, probs, x2, mod_l, ln_g, ln_b, seq):
    t, d = x2.shape
    tiles_per_seq = seq // COMBINE_TM
    return pl.pallas_call(
        _combine_kernel,
        grid_spec=pltpu.PrefetchScalarGridSpec(
            num_scalar_prefetch=1,
            grid=(t // COMBINE_TM,),
            in_specs=[
                pl.BlockSpec(memory_space=pl.ANY),
                pl.BlockSpec((COMBINE_TM, TOP_K), lambda i, dst: (i, 0)),
                pl.BlockSpec((COMBINE_TM, d), lambda i, dst: (i, 0)),
                pl.BlockSpec((None, N_MODULATIONS, d), lambda i, dst: (i // tiles_per_seq, 0, 0)),
                pl.BlockSpec((1, d), lambda i, dst: (0, 0)),
                pl.BlockSpec((1, d), lambda i, dst: (0, 0)),
            ],
            out_specs=pl.BlockSpec((COMBINE_TM, d), lambda i, dst: (i, 0)),
            scratch_shapes=[pltpu.VMEM((TOP_K, COMBINE_TM, d), F32), pltpu.SemaphoreType.DMA(())],
        ),
        out_shape=jax.ShapeDtypeStruct((t, d), F32),
        compiler_params=_cparams(("arbitrary",)),
        name="combine_ln",
    )(dest, y, probs, x2, mod_l, ln_g.reshape(1, d), ln_b.reshape(1, d))


def _routing_tables(top_idx):
    t = top_idx.shape[0]
    flat_e = top_idx.reshape(-1)
    flat_t = jnp.repeat(jnp.arange(t, dtype=jnp.int32), TOP_K)
    onehot = jax.nn.one_hot(flat_e, N_EXPERTS, dtype=jnp.int32)
    counts = jnp.sum(onehot, axis=0)
    rank = jnp.take_along_axis(jnp.cumsum(onehot, axis=0) - onehot, flat_e[:, None], axis=1)[:, 0]
    padded = (counts + MOE_TM - 1) // MOE_TM * MOE_TM
    ends = jnp.cumsum(padded)
    starts = ends - padded
    dest = (starts[flat_e] + rank).astype(jnp.int32)
    n_rows = t * TOP_K + N_EXPERTS * MOE_TM
    row_tok = jnp.zeros((n_rows,), jnp.int32).at[dest].set(flat_t)
    tile_start = jnp.arange(n_rows // MOE_TM, dtype=jnp.int32) * MOE_TM
    tile_e = jnp.minimum(jnp.sum(tile_start[:, None] >= ends[None, :], axis=1), N_EXPERTS - 1).astype(jnp.int32)
    n_used = (ends[-1:] // MOE_TM).astype(jnp.int32)
    return dest, row_tok, tile_e, n_used


def _moe_block(h, logits, x2, mod_l, ln_g, ln_b, w_in, b_in, w_out, b_out, layer, seq):
    top_idx, probs = _route_topk(logits)
    dest, row_tok, tile_e, n_used = _routing_tables(top_idx)
    xs = _moe_gather(h, row_tok)
    hid = _expert_ffn1(xs, tile_e, n_used, w_in, b_in, layer)
    y = _expert_ffn2(hid, tile_e, n_used, w_out, b_out, layer)
    return _combine_ln(y, dest, probs, x2, mod_l, ln_g, ln_b, seq)


def kernel(x, c, positions, cond_w, cond_b, ln_g, ln_b, attn_w_qkv, attn_w_o, sg_w_in, sg_b_in, sg_ln_g, sg_ln_b, sg_w_spatial, sg_b_spatial, sg_w_out, router_w, router_b, expert_w_in, expert_b_in, expert_w_out, expert_b_out):
    bsz, seq, d = x.shape
    x2 = x.reshape(bsz * seq, d)
    mod = _modulation(c, cond_w, cond_b).reshape(DEPTH, bsz, N_MODULATIONS, d)
    tabs = _rope_tables(positions)

    qkv = _qkv_projection(x2, mod[0], attn_w_qkv[0].astype(BF16), tabs, seq)
    outs, lses = [], []
    for g, (_, dilation) in enumerate(DILATED_GROUPS):
        o, lse = _dilated_attention(qkv, bsz, seq, g, dilation)
        outs.append(o)
        lses.append(lse)
    mixed = _merge_groups(outs, lses)
    x2, h, logits = _proj_ln(mixed, attn_w_o[0].astype(BF16), x2, mod[0], ln_g[0, 0], ln_b[0, 0],
                             router_w[0], router_b[0], seq)
    x2 = _moe_block(h, logits, x2, mod[0], ln_g[0, 1], ln_b[0, 1], expert_w_in, expert_b_in[0],
                    expert_w_out, expert_b_out[0], 0, seq)

    z = _sg_in_projection(x2, mod[1], sg_w_in[0].astype(BF16), sg_b_in[0], seq)
    gated = _spatial_gating(z, sg_ln_g[0], sg_ln_b[0], sg_w_spatial[0], sg_b_spatial[0])
    x2, h, logits = _proj_ln(gated, sg_w_out[0].astype(BF16), x2, mod[1], ln_g[1, 0], ln_b[1, 0],
                             router_w[1], router_b[1], seq)
    x2 = _moe_block(h, logits, x2, mod[1], ln_g[1, 1], ln_b[1, 1], expert_w_in, expert_b_in[1],
                    expert_w_out, expert_b_out[1], 1, seq)
    return x2.reshape(bsz, seq, d)
```

```python
import functools

import jax
import jax.numpy as jnp
from jax import lax
from jax.experimental import pallas as pl
from jax.experimental.pallas import tpu as pltpu

F32 = jnp.float32
BF16 = jnp.bfloat16

D_MODEL = 2048
DEPTH = 2
HEAD_DIM = 64
HEADS_PER_GROUP = 16
DILATED_GROUPS = ((128, 1), (512, 4), (2048, 16))
GROUP_WIDTH = HEADS_PER_GROUP * HEAD_DIM
ATT_WIDTH = GROUP_WIDTH * len(DILATED_GROUPS)
ROPE_THETA = 500000.0
ROT_DIM = HEAD_DIM // 4
ATT_BLOCK = 128

SG_CHUNK = 128
SG_WIDTH = 2 * D_MODEL
SG_GROUPS = 16
SG_GROUP_DIM = SG_WIDTH // SG_GROUPS

N_EXPERTS = 32
TOP_K = 4
EXPERT_FF = D_MODEL
SWIGLU_LIMIT = 7.0
SWIGLU_ALPHA = 1.702

N_MODULATIONS = 6
DEEPNORM_ALPHA = (2 * DEPTH) ** 0.25
LN_EPS = 1e-5
NEG_BIG = -1e30

LANES = 128
TOKEN_CHUNKS = D_MODEL // LANES
VMEM_LIMIT_BYTES = 56 * 1024 * 1024

MOD_TN = 512
ROPE_TM = 1024
PROJ_TM = 1024
PROJ_TN = 1024
MERGE_TM = 512
LN_TM = 256
TOPK_TM = 1024
SGU_TM = 256
MOE_TM = 256
FFN1_TF = 512
FFN2_TN = 1024
GATHER_ROWS = 512
COMBINE_TM = 256


def _cparams(sem):
    return pltpu.CompilerParams(dimension_semantics=sem, vmem_limit_bytes=VMEM_LIMIT_BYTES)


def _store_token_major(ref, val):
    rows = val.shape[0]
    for c in range(TOKEN_CHUNKS):
        ref[pl.ds(c, rows, stride=TOKEN_CHUNKS), :] = val[:, c * LANES:(c + 1) * LANES]


def _load_token_major(ref, rows):
    return jnp.concatenate(
        [ref[pl.ds(c, rows, stride=TOKEN_CHUNKS), :] for c in range(TOKEN_CHUNKS)], axis=1)


def _layer_norm(z, g, b):
    mu = jnp.mean(z, axis=-1, keepdims=True)
    zc = z - mu
    var = jnp.mean(zc * zc, axis=-1, keepdims=True)
    return zc * lax.rsqrt(var + LN_EPS) * g + b


def _mod_kernel(ct_ref, w_ref, b_ref, o_ref):
    ct = ct_ref[...]
    ca = ct * jax.nn.sigmoid(ct)
    w = w_ref[...]
    for b in range(ct.shape[1]):
        o_ref[b:b + 1, :] = jnp.sum(ca[:, b:b + 1] * w, axis=0, keepdims=True) + b_ref[...]


def _modulation(c, cond_w, cond_b):
    nl, d, n = cond_w.shape
    bsz = c.shape[0]
    return pl.pallas_call(
        _mod_kernel,
        grid=(nl, n // MOD_TN),
        in_specs=[
            pl.BlockSpec((d, bsz), lambda l, j: (0, 0)),
            pl.BlockSpec((None, d, MOD_TN), lambda l, j: (l, 0, j)),
            pl.BlockSpec((None, 1, MOD_TN), lambda l, j: (l, 0, j)),
        ],
        out_specs=pl.BlockSpec((None, bsz, MOD_TN), lambda l, j: (l, 0, j)),
        out_shape=jax.ShapeDtypeStruct((nl, bsz, n), F32),
        compiler_params=_cparams(("parallel", "parallel")),
        name="modulation",
    )(c.T, cond_w, cond_b.reshape(nl, 1, n))


def _rope_kernel(pos_ref, inv_ref, c_ref, s1_ref, s2_ref):
    ang = pos_ref[...] * inv_ref[...]
    lane = lax.broadcasted_iota(jnp.int32, ang.shape, 1) & (HEAD_DIM - 1)
    cs = jnp.cos(ang)
    sn = jnp.sin(ang)
    half = ROT_DIM // 2
    c_ref[...] = jnp.where(lane < ROT_DIM, cs, 1.0)
    s1_ref[...] = jnp.where(lane < half, -sn, 0.0)
    s2_ref[...] = jnp.where((lane >= half) & (lane < ROT_DIM), sn, 0.0)


def _rope_tables(positions):
    t = positions.size
    pos = positions.astype(F32).reshape(t, 1)
    inv = jnp.power(jnp.float32(ROPE_THETA), -jnp.arange(0, ROT_DIM, 2, dtype=F32) / ROT_DIM)
    lane = jnp.arange(LANES) % HEAD_DIM
    inv_row = inv[lane % (ROT_DIM // 2)].reshape(1, LANES)
    spec = pl.BlockSpec((ROPE_TM, LANES), lambda i: (i, 0))
    shp = jax.ShapeDtypeStruct((t, LANES), F32)
    return pl.pallas_call(
        _rope_kernel,
        grid=(t // ROPE_TM,),
        in_specs=[pl.BlockSpec((ROPE_TM, 1), lambda i: (i, 0)), pl.BlockSpec((1, LANES), lambda i: (0, 0))],
        out_specs=[spec, spec, spec],
        out_shape=[shp, shp, shp],
        compiler_params=_cparams(("parallel",)),
        name="rope_tables",
    )(pos, inv_row)


def _modulate_into(h_scr, x_ref, mod_ref):
    shift = mod_ref[0:1, :]
    scale = mod_ref[1:2, :]
    h_scr[...] = (x_ref[...] * (1.0 + scale) + shift).astype(BF16)


def _qkv_kernel(x_ref, mod_ref, w_ref, c_ref, s1_ref, s2_ref, o_ref, h_scr, r_scr, *, dilation):
    j = pl.program_id(1)

    @pl.when(j == 0)
    def _():
        _modulate_into(h_scr, x_ref, mod_ref)

    acc = jnp.dot(h_scr[...], w_ref[...], preferred_element_type=F32)
    n_col_blocks = acc.shape[1] // LANES

    @pl.when(j < 2)
    def _():
        c = c_ref[...]
        s1 = s1_ref[...]
        s2 = s2_ref[...]
        for cb in range(n_col_blocks):
            a = acc[:, cb * LANES:(cb + 1) * LANES]
            up = pltpu.roll(a, LANES - ROT_DIM // 2, 1)
            dn = pltpu.roll(a, ROT_DIM // 2, 1)
            r_scr[cb] = a * c + up * s1 + dn * s2

    @pl.when(j == 2)
    def _():
        for cb in range(n_col_blocks):
            r_scr[cb] = acc[:, cb * LANES:(cb + 1) * LANES]

    rows = r_scr.shape[1] // dilation
    for r in range(dilation):
        for cb in range(n_col_blocks):
            o_ref[r, :, cb * LANES:(cb + 1) * LANES] = (
                r_scr[cb, pl.ds(r, rows, stride=dilation), :].astype(BF16))


def _qkv_projection(x2, mod_l, w_bf, tabs, bsz, seq, dilation):
    t, d = x2.shape
    tiles_per_seq = seq // PROJ_TM
    tab_spec = pl.BlockSpec((PROJ_TM, LANES), lambda i, j: (i, 0))
    return pl.pallas_call(
        functools.partial(_qkv_kernel, dilation=dilation),
        grid=(t // PROJ_TM, 3),
        in_specs=[
            pl.BlockSpec((PROJ_TM, d), lambda i, j: (i, 0)),
            pl.BlockSpec((None, N_MODULATIONS, d), lambda i, j: (i // tiles_per_seq, 0, 0)),
            pl.BlockSpec((d, GROUP_WIDTH), lambda i, j: (0, j)),
            tab_spec, tab_spec, tab_spec,
        ],
        out_specs=pl.BlockSpec((None, dilation, PROJ_TM // dilation, GROUP_WIDTH),
                               lambda i, j: (i // tiles_per_seq, 0, i % tiles_per_seq, j)),
        out_shape=jax.ShapeDtypeStruct((bsz, dilation, seq // dilation, 3 * GROUP_WIDTH), BF16),
        scratch_shapes=[pltpu.VMEM((PROJ_TM, d), BF16),
                        pltpu.VMEM((GROUP_WIDTH // LANES, PROJ_TM, LANES), F32)],
        compiler_params=_cparams(("arbitrary", "arbitrary")),
        name=f"qkv_projection_d{dilation}",
    )(x2, mod_l, w_bf, *tabs)


def _sg_in_kernel(x_ref, mod_ref, w_ref, b_ref, o_ref, h_scr):
    @pl.when(pl.program_id(1) == 0)
    def _():
        _modulate_into(h_scr, x_ref, mod_ref)

    z = jnp.dot(h_scr[...], w_ref[...], preferred_element_type=F32) + b_ref[...]
    o_ref[...] = (0.5 * z * (1.0 + lax.erf(z * (2.0 ** -0.5)))).astype(BF16)


def _sg_in_projection(x2, mod_l, w_bf, b_in, seq):
    t, d = x2.shape
    n = w_bf.shape[1]
    tiles_per_seq = seq // PROJ_TM
    return pl.pallas_call(
        _sg_in_kernel,
        grid=(t // PROJ_TM, n // PROJ_TN),
        in_specs=[
            pl.BlockSpec((PROJ_TM, d), lambda i, j: (i, 0)),
            pl.BlockSpec((None, N_MODULATIONS, d), lambda i, j: (i // tiles_per_seq, 0, 0)),
            pl.BlockSpec((d, PROJ_TN), lambda i, j: (0, j)),
            pl.BlockSpec((1, PROJ_TN), lambda i, j: (0, j)),
        ],
        out_specs=pl.BlockSpec((PROJ_TM, PROJ_TN), lambda i, j: (i, j)),
        out_shape=jax.ShapeDtypeStruct((t, n), BF16),
        scratch_shapes=[pltpu.VMEM((PROJ_TM, d), BF16)],
        compiler_params=_cparams(("arbitrary", "arbitrary")),
        name="sg_in_projection",
    )(x2, mod_l, w_bf, b_in.reshape(1, n))


def _attn_kernel(q_ref, kp_ref, kc_ref, vp_ref, vc_ref, o_ref, lse_ref):
    n = pl.program_id(2)
    blk = q_ref.shape[0]
    row = lax.broadcasted_iota(jnp.int32, (blk, blk), 0)
    col = lax.broadcasted_iota(jnp.int32, (blk, blk), 1)
    prev_ok = (col >= row) & (n > 0)
    cur_ok = col <= row
    lane = lax.broadcasted_iota(jnp.int32, (1, LANES), 1)
    low = lane < HEAD_DIM
    keep = [jnp.where(low, 1.0, 0.0).astype(BF16), jnp.where(low, 0.0, 1.0).astype(BF16)]
    low_full = lax.broadcasted_iota(jnp.int32, (blk, LANES), 1) < HEAD_DIM
    dn = (((1,), (1,)), ((), ()))
    for pair in range(GROUP_WIDTH // LANES):
        cs = slice(pair * LANES, (pair + 1) * LANES)
        q = q_ref[:, cs] * (HEAD_DIM ** -0.5)
        kp, kc, vp, vc = kp_ref[:, cs], kc_ref[:, cs], vp_ref[:, cs], vc_ref[:, cs]
        o_half, lse_half = [], []
        for half in range(2):
            mine, other = keep[half], keep[1 - half]
            sp = lax.dot_general(q, kp * mine, dn, preferred_element_type=F32)
            sc = lax.dot_general(q, kc * mine, dn, preferred_element_type=F32)
            sp = jnp.where(prev_ok, sp, NEG_BIG)
            sc = jnp.where(cur_ok, sc, NEG_BIG)
            m = jnp.max(jnp.maximum(sp, sc), axis=-1, keepdims=True)
            pp = jnp.exp(sp - m).astype(BF16)
            pc = jnp.exp(sc - m).astype(BF16)
            acc = jnp.dot(pp, vp * mine + other, preferred_element_type=F32)
            acc = acc + jnp.dot(pc, vc * mine + other, preferred_element_type=F32)
            den = pltpu.roll(acc, HEAD_DIM, 1)
            o_half.append(acc / den)
            lse_half.append(m + jnp.log(den))
        o_ref[:, cs] = jnp.where(low_full, o_half[0], o_half[1]).astype(BF16)
        lse_ref[:, cs] = jnp.where(low_full, lse_half[0], lse_half[1])


def _dilated_attention(qkv, dilation):
    bsz, _, length, _ = qkv.shape
    nblk = length // ATT_BLOCK

    def spec(part, prev):
        def index(b, r, n):
            return (b, r, jnp.maximum(n - 1, 0) if prev else n, part)
        return pl.BlockSpec((None, None, ATT_BLOCK, GROUP_WIDTH), index)

    out_spec = pl.BlockSpec((None, None, ATT_BLOCK, GROUP_WIDTH), lambda b, r, n: (b, r, n, 0))
    return pl.pallas_call(
        _attn_kernel,
        grid=(bsz, dilation, nblk),
        in_specs=[spec(0, False), spec(1, True), spec(1, False), spec(2, True), spec(2, False)],
        out_specs=[out_spec, out_spec],
        out_shape=[jax.ShapeDtypeStruct((bsz, dilation, length, GROUP_WIDTH), BF16),
                   jax.ShapeDtypeStruct((bsz, dilation, length, GROUP_WIDTH), F32)],
        compiler_params=_cparams(("parallel", "parallel", "parallel")),
        name=f"dilated_attention_d{dilation}",
    )(qkv, qkv, qkv, qkv, qkv)


def _merge_kernel(*refs):
    ng = len(DILATED_GROUPS)
    o_refs, l_refs, out_ref = refs[:ng], refs[ng:2 * ng], refs[2 * ng]
    scratch = refs[2 * ng + 1:]
    outs, lses = [], []
    for g, (_, dilation) in enumerate(DILATED_GROUPS):
        if dilation == 1:
            outs.append(o_refs[g][0].astype(F32))
            lses.append(l_refs[g][0])
            continue
        so, sl = scratch[2 * (g - 1)], scratch[2 * (g - 1) + 1]
        n_col_blocks = so.shape[0]
        rows = so.shape[1] // dilation
        for r in range(dilation):
            for cb in range(n_col_blocks):
                cs = slice(cb * LANES, (cb + 1) * LANES)
                so[cb, pl.ds(r, rows, stride=dilation), :] = o_refs[g][r, :, cs].astype(F32)
                sl[cb, pl.ds(r, rows, stride=dilation), :] = l_refs[g][r, :, cs]
        outs.append(jnp.concatenate([so[cb] for cb in range(n_col_blocks)], axis=1))
        lses.append(jnp.concatenate([sl[cb] for cb in range(n_col_blocks)], axis=1))
    mx = jnp.maximum(jnp.maximum(lses[0], lses[1]), lses[2])
    es = [jnp.exp(l - mx) for l in lses]
    inv = 1.0 / (es[0] + es[1] + es[2])
    for g in range(ng):
        out_ref[:, g * GROUP_WIDTH:(g + 1) * GROUP_WIDTH] = (outs[g] * (es[g] * inv)).astype(BF16)


def _merge_groups(outs, lses, seq):
    bsz = outs[0].shape[0]
    tiles_per_seq = seq // MERGE_TM
    specs = [pl.BlockSpec((None, dilation, MERGE_TM // dilation, GROUP_WIDTH),
                          lambda i: (i // tiles_per_seq, 0, i % tiles_per_seq, 0))
             for _, dilation in DILATED_GROUPS]
    assert DILATED_GROUPS[0][1] == 1
    scratch = []
    for _ in DILATED_GROUPS[1:]:
        scratch += [pltpu.VMEM((GROUP_WIDTH // LANES, MERGE_TM, LANES), F32)] * 2
    return pl.pallas_call(
        _merge_kernel,
        grid=(bsz * tiles_per_seq,),
        in_specs=specs + specs,
        out_specs=pl.BlockSpec((MERGE_TM, ATT_WIDTH), lambda i: (i, 0)),
        out_shape=jax.ShapeDtypeStruct((bsz * seq, ATT_WIDTH), BF16),
        scratch_shapes=scratch,
        compiler_params=_cparams(("parallel",)),
        name="merge_groups",
    )(*outs, *lses)


def _sgu_kernel(u_ref, v_ref, g_ref, b_ref, wsp_ref, bsp_ref, o_ref):
    v = v_ref[...].astype(F32)
    vn = _layer_norm(v, g_ref[...], b_ref[...]).astype(BF16)
    row = lax.broadcasted_iota(jnp.int32, (SG_CHUNK, SG_CHUNK), 0)
    col = lax.broadcasted_iota(jnp.int32, (SG_CHUNK, SG_CHUNK), 1)
    causal = col <= row
    for g in range(SG_GROUPS):
        w = jnp.where(causal, wsp_ref[g], 0.0).astype(BF16)
        bias = bsp_ref[:, g:g + 1]
        cs = slice(g * SG_GROUP_DIM, (g + 1) * SG_GROUP_DIM)
        for c in range(u_ref.shape[0] // SG_CHUNK):
            rs = slice(c * SG_CHUNK, (c + 1) * SG_CHUNK)
            mixed = jnp.dot(w, vn[rs, cs], preferred_element_type=F32) + bias
            o_ref[rs, cs] = (u_ref[rs, cs].astype(F32) * mixed).astype(BF16)


def _spatial_gating(z, ln_g, ln_b, w_sp, b_sp):
    t = z.shape[0]
    return pl.pallas_call(
        _sgu_kernel,
        grid=(t // SGU_TM,),
        in_specs=[
            pl.BlockSpec((SGU_TM, SG_WIDTH), lambda i: (i, 0)),
            pl.BlockSpec((SGU_TM, SG_WIDTH), lambda i: (i, 1)),
            pl.BlockSpec((1, SG_WIDTH), lambda i: (0, 0)),
            pl.BlockSpec((1, SG_WIDTH), lambda i: (0, 0)),
            pl.BlockSpec((SG_GROUPS, SG_CHUNK, SG_CHUNK), lambda i: (0, 0, 0)),
            pl.BlockSpec((SG_CHUNK, SG_GROUPS), lambda i: (0, 0)),
        ],
        out_specs=pl.BlockSpec((SGU_TM, SG_WIDTH), lambda i: (i, 0)),
        out_shape=jax.ShapeDtypeStruct((t, SG_WIDTH), BF16),
        compiler_params=_cparams(("parallel",)),
        name="spatial_gating",
    )(z, z, ln_g.reshape(1, -1), ln_b.reshape(1, -1), w_sp, b_sp.T)


def _proj_ln_kernel(a_ref, w_ref, x_ref, mod_ref, g_ref, b_ref, rw_hi_ref, rw_lo_ref, rb_ref,
                    x_out, h_out, logit_out):
    y = jnp.dot(a_ref[...], w_ref[...], preferred_element_type=F32)
    gate = mod_ref[2:3, :]
    xn = _layer_norm(DEEPNORM_ALPHA * x_ref[...] + (1.0 + gate) * y, g_ref[...], b_ref[...])
    x_out[...] = xn
    h = xn * (1.0 + mod_ref[4:5, :]) + mod_ref[3:4, :]
    _store_token_major(h_out, h)
    h_hi = h.astype(BF16)
    h_lo = (h - h_hi.astype(F32)).astype(BF16)
    rw_hi = rw_hi_ref[...]
    logits = jnp.dot(h_hi, rw_hi, preferred_element_type=F32)
    logits = logits + jnp.dot(h_hi, rw_lo_ref[...], preferred_element_type=F32)
    logits = logits + jnp.dot(h_lo, rw_hi, preferred_element_type=F32)
    logit_out[...] = logits + rb_ref[...]


def _proj_ln(a, w_bf, x2, mod_l, ln_g, ln_b, router_w, router_b, seq):
    t, d = x2.shape
    k = a.shape[1]
    tiles_per_seq = seq // LN_TM
    rw_hi = router_w.astype(BF16)
    rw_lo = (router_w - rw_hi.astype(F32)).astype(BF16)
    row = lambda i: (i, 0)
    fixed = lambda i: (0, 0)
    return pl.pallas_call(
        _proj_ln_kernel,
        grid=(t // LN_TM,),
        in_specs=[
            pl.BlockSpec((LN_TM, k), row),
            pl.BlockSpec((k, d), fixed, pipeline_mode=pl.Buffered(1)),
            pl.BlockSpec((LN_TM, d), row),
            pl.BlockSpec((None, N_MODULATIONS, d), lambda i: (i // tiles_per_seq, 0, 0)),
            pl.BlockSpec((1, d), fixed),
            pl.BlockSpec((1, d), fixed),
            pl.BlockSpec((d, N_EXPERTS), fixed),
            pl.BlockSpec((d, N_EXPERTS), fixed),
            pl.BlockSpec((1, N_EXPERTS), fixed),
        ],
        out_specs=[pl.BlockSpec((LN_TM, d), row), pl.BlockSpec((LN_TM * TOKEN_CHUNKS, LANES), row),
                   pl.BlockSpec((LN_TM, N_EXPERTS), row)],
        out_shape=[jax.ShapeDtypeStruct((t, d), F32), jax.ShapeDtypeStruct((t * TOKEN_CHUNKS, LANES), F32),
                   jax.ShapeDtypeStruct((t, N_EXPERTS), F32)],
        compiler_params=_cparams(("parallel",)),
        name="proj_ln",
    )(a, w_bf, x2, mod_l, ln_g.reshape(1, d), ln_b.reshape(1, d), rw_hi, rw_lo,
      router_b.reshape(1, N_EXPERTS))


def _topk_kernel(l_ref, idx_ref, p_ref):
    l = l_ref[...]
    tm = l.shape[0]
    lane = lax.broadcasted_iota(jnp.int32, l.shape, 1)
    out_lane = lax.broadcasted_iota(jnp.int32, (tm, TOP_K), 1)
    idx_out = jnp.zeros((tm, TOP_K), jnp.int32)
    val_out = jnp.zeros((tm, TOP_K), F32)
    for k in range(TOP_K):
        m = jnp.max(l, axis=-1, keepdims=True)
        idx = jnp.min(jnp.where(l == m, lane, N_EXPERTS), axis=-1, keepdims=True)
        idx_out = jnp.where(out_lane == k, idx, idx_out)
        val_out = jnp.where(out_lane == k, m, val_out)
        l = jnp.where(lane == idx, -jnp.inf, l)
    e = jnp.exp(val_out - jnp.max(val_out, axis=-1, keepdims=True))
    idx_ref[...] = idx_out
    p_ref[...] = e / jnp.sum(e, axis=-1, keepdims=True)


def _route_topk(logits):
    t = logits.shape[0]
    return pl.pallas_call(
        _topk_kernel,
        grid=(t // TOPK_TM,),
        in_specs=[pl.BlockSpec((TOPK_TM, N_EXPERTS), lambda i: (i, 0))],
        out_specs=[pl.BlockSpec((TOPK_TM, TOP_K), lambda i: (i, 0))] * 2,
        out_shape=[jax.ShapeDtypeStruct((t, TOP_K), jnp.int32), jax.ShapeDtypeStruct((t, TOP_K), F32)],
        compiler_params=_cparams(("parallel",)),
        name="route_topk",
    )(logits)


def _slab_copy(src_hbm, dst_vmem, sem, src_tok, dst_tok):
    src = pl.multiple_of(src_tok * TOKEN_CHUNKS, TOKEN_CHUNKS)
    dst = pl.multiple_of(dst_tok * TOKEN_CHUNKS, TOKEN_CHUNKS)
    return pltpu.make_async_copy(src_hbm.at[pl.ds(src, TOKEN_CHUNKS)], dst_vmem.at[pl.ds(dst, TOKEN_CHUNKS)], sem)


def _gather_kernel(tok_ref, h_hbm, o_ref, buf, sem):
    i = pl.program_id(0)
    n_steps = pl.num_programs(0)

    def start_all(step, slot):
        def body(r, carry):
            _slab_copy(h_hbm, buf.at[slot], sem.at[slot], tok_ref[step * GATHER_ROWS + r], r).start()
            return carry
        lax.fori_loop(0, GATHER_ROWS, body, 0, unroll=8)

    def wait_all(slot):
        def body(r, carry):
            _slab_copy(h_hbm, buf.at[slot], sem.at[slot], 0, r).wait()
            return carry
        lax.fori_loop(0, GATHER_ROWS, body, 0, unroll=8)

    @pl.when(i == 0)
    def _():
        start_all(0, 0)

    for slot in range(2):
        @pl.when((i % 2 == slot) & (i + 1 < n_steps))
        def _():
            start_all(i + 1, 1 - slot)

        @pl.when(i % 2 == slot)
        def _():
            wait_all(slot)
            o_ref[...] = _load_token_major(buf.at[slot], GATHER_ROWS).astype(BF16)


def _moe_gather(h_tm, row_tok):
    n_rows = row_tok.shape[0]
    return pl.pallas_call(
        _gather_kernel,
        grid_spec=pltpu.PrefetchScalarGridSpec(
            num_scalar_prefetch=1,
            grid=(n_rows // GATHER_ROWS,),
            in_specs=[pl.BlockSpec(memory_space=pl.ANY)],
            out_specs=pl.BlockSpec((GATHER_ROWS, D_MODEL), lambda i, tok: (i, 0)),
            scratch_shapes=[pltpu.VMEM((2, GATHER_ROWS * TOKEN_CHUNKS, LANES), F32),
                            pltpu.SemaphoreType.DMA((2,))],
        ),
        out_shape=jax.ShapeDtypeStruct((n_rows, D_MODEL), BF16),
        compiler_params=_cparams(("arbitrary",)),
        name="moe_gather",
    )(row_tok, h_tm)


def _tile_state(te_ref, nu_ref):
    i = pl.program_id(1)
    n_used = nu_ref[0]
    cur = te_ref[jnp.minimum(i, n_used - 1)]
    prev = te_ref[jnp.minimum(jnp.maximum(i - 1, 0), n_used - 1)]
    active = i < n_used
    fresh = active & ((i == 0) | (cur != prev))
    return active, fresh


def _ffn1_kernel(te_ref, nu_ref, x_ref, wg_ref, wu_ref, bg_ref, bu_ref, o_ref, wg_bf, wu_bf):
    active, fresh = _tile_state(te_ref, nu_ref)

    @pl.when(fresh)
    def _():
        wg_bf[...] = wg_ref[...].astype(BF16)
        wu_bf[...] = wu_ref[...].astype(BF16)

    @pl.when(active)
    def _():
        x = x_ref[...]
        g = jnp.dot(x, wg_bf[...], preferred_element_type=F32) + bg_ref[...]
        u = jnp.dot(x, wu_bf[...], preferred_element_type=F32) + bu_ref[...]
        gate = jnp.minimum(g, SWIGLU_LIMIT)
        up = jnp.clip(u, -SWIGLU_LIMIT, SWIGLU_LIMIT)
        o_ref[...] = ((up + 1.0) * gate * jax.nn.sigmoid(SWIGLU_ALPHA * gate)).astype(BF16)

    @pl.when(jnp.logical_not(active))
    def _():
        o_ref[...] = jnp.zeros_like(o_ref)


def _expert_ffn1(xs, tile_e, n_used, w_in, b_in, layer):
    n_rows, d = xs.shape
    _, ne, _, two_ff = w_in.shape
    ff = two_ff // 2
    nf = ff // FFN1_TF
    nt = n_rows // MOE_TM

    def tile(i, nu):
        return jnp.minimum(i, nu[0] - 1)

    return pl.pallas_call(
        _ffn1_kernel,
        grid_spec=pltpu.PrefetchScalarGridSpec(
            num_scalar_prefetch=2,
            grid=(nf, nt),
            in_specs=[
                pl.BlockSpec((MOE_TM, d), lambda f, i, te, nu: (tile(i, nu), 0)),
                pl.BlockSpec((None, None, d, FFN1_TF), lambda f, i, te, nu: (layer, te[tile(i, nu)], 0, f)),
                pl.BlockSpec((None, None, d, FFN1_TF), lambda f, i, te, nu: (layer, te[tile(i, nu)], 0, nf + f)),
                pl.BlockSpec((None, 1, FFN1_TF), lambda f, i, te, nu: (te[tile(i, nu)], 0, f)),
                pl.BlockSpec((None, 1, FFN1_TF), lambda f, i, te, nu: (te[tile(i, nu)], 0, nf + f)),
            ],
            out_specs=pl.BlockSpec((MOE_TM, FFN1_TF), lambda f, i, te, nu: (i, f)),
            scratch_shapes=[pltpu.VMEM((d, FFN1_TF), BF16), pltpu.VMEM((d, FFN1_TF), BF16)],
        ),
        out_shape=jax.ShapeDtypeStruct((n_rows, ff), BF16),
        compiler_params=_cparams(("arbitrary", "arbitrary")),
        name="expert_ffn1",
    )(tile_e, n_used, xs, w_in, w_in, b_in.reshape(ne, 1, two_ff), b_in.reshape(ne, 1, two_ff))


def _ffn2_kernel(te_ref, nu_ref, h_ref, w_ref, b_ref, o_ref, w_bf):
    active, fresh = _tile_state(te_ref, nu_ref)

    @pl.when(fresh)
    def _():
        w_bf[...] = w_ref[...].astype(BF16)

    @pl.when(active)
    def _():
        y = jnp.dot(h_ref[...], w_bf[...], preferred_element_type=F32) + b_ref[...]
        _store_token_major(o_ref, y)

    @pl.when(jnp.logical_not(active))
    def _():
        o_ref[...] = jnp.zeros_like(o_ref)


def _expert_ffn2(hid, tile_e, n_used, w_out, b_out, layer):
    n_rows, ff = hid.shape
    _, ne, _, d = w_out.shape
    nt = n_rows // MOE_TM

    def tile(i, nu):
        return jnp.minimum(i, nu[0] - 1)

    return pl.pallas_call(
        _ffn2_kernel,
        grid_spec=pltpu.PrefetchScalarGridSpec(
            num_scalar_prefetch=2,
            grid=(1, nt),
            in_specs=[
                pl.BlockSpec((MOE_TM, ff), lambda n, i, te, nu: (tile(i, nu), 0)),
                pl.BlockSpec((None, None, ff, d), lambda n, i, te, nu: (layer, te[tile(i, nu)], 0, 0)),
                pl.BlockSpec((None, 1, d), lambda n, i, te, nu: (te[tile(i, nu)], 0, 0)),
            ],
            out_specs=pl.BlockSpec((MOE_TM * TOKEN_CHUNKS, LANES), lambda n, i, te, nu: (i, 0)),
            scratch_shapes=[pltpu.VMEM((ff, d), BF16)],
        ),
        out_shape=jax.ShapeDtypeStruct((n_rows * TOKEN_CHUNKS, LANES), F32),
        compiler_params=_cparams(("arbitrary", "arbitrary")),
        name="expert_ffn2",
    )(tile_e, n_used, hid, w_out, b_out.reshape(ne, 1, d))


def _combine_kernel(dest_ref, y_hbm, p_ref, x_ref, mod_ref, g_ref, b_ref, o_ref, buf, sem):
    i = pl.program_id(0)
    n_steps = pl.num_programs(0)

    def start_all(step, slot):
        def body(r, carry):
            for k in range(TOP_K):
                row = dest_ref[(step * COMBINE_TM + r) * TOP_K + k]
                _slab_copy(y_hbm, buf.at[slot, k], sem.at[slot], row, r).start()
            return carry
        lax.fori_loop(0, COMBINE_TM, body, 0, unroll=2)

    def wait_all(slot):
        def body(r, carry):
            for k in range(TOP_K):
                _slab_copy(y_hbm, buf.at[slot, k], sem.at[slot], 0, r).wait()
            return carry
        lax.fori_loop(0, COMBINE_TM, body, 0, unroll=2)

    @pl.when(i == 0)
    def _():
        start_all(0, 0)

    for slot in range(2):
        @pl.when((i % 2 == slot) & (i + 1 < n_steps))
        def _():
            start_all(i + 1, 1 - slot)

        @pl.when(i % 2 == slot)
        def _():
            wait_all(slot)
            p = p_ref[...]
            moe = p[:, 0:1] * _load_token_major(buf.at[slot, 0], COMBINE_TM)
            for k in range(1, TOP_K):
                moe = moe + p[:, k:k + 1] * _load_token_major(buf.at[slot, k], COMBINE_TM)
            gate = mod_ref[5:6, :]
            o_ref[...] = _layer_norm(DEEPNORM_ALPHA * x_ref[...] + (1.0 + gate) * moe, g_ref[...], b_ref[...])


def _combine_ln(y, dest, probs, x2, mod_l, ln_g, ln_b, seq):
    t, d = x2.shape
    tiles_per_seq = seq // COMBINE_TM
    return pl.pallas_call(
        _combine_kernel,
        grid_spec=pltpu.PrefetchScalarGridSpec(
            num_scalar_prefetch=1,
            grid=(t // COMBINE_TM,),
            in_specs=[
                pl.BlockSpec(memory_space=pl.ANY),
                pl.BlockSpec((COMBINE_TM, TOP_K), lambda i, dst: (i, 0)),
                pl.BlockSpec((COMBINE_TM, d), lambda i, dst: (i, 0)),
                pl.BlockSpec((None, N_MODULATIONS, d), lambda i, dst: (i // tiles_per_seq, 0, 0)),
                pl.BlockSpec((1, d), lambda i, dst: (0, 0)),
                pl.BlockSpec((1, d), lambda i, dst: (0, 0)),
            ],
            out_specs=pl.BlockSpec((COMBINE_TM, d), lambda i, dst: (i, 0)),
            scratch_shapes=[pltpu.VMEM((2, TOP_K, COMBINE_TM * TOKEN_CHUNKS, LANES), F32),
                            pltpu.SemaphoreType.DMA((2,))],
        ),
        out_shape=jax.ShapeDtypeStruct((t, d), F32),
        compiler_params=_cparams(("arbitrary",)),
        name="combine_ln",
    )(dest, y, probs, x2, mod_l, ln_g.reshape(1, d), ln_b.reshape(1, d))


def _routing_tables(top_idx):
    t = top_idx.shape[0]
    flat_e = top_idx.reshape(-1)
    flat_t = jnp.repeat(jnp.arange(t, dtype=jnp.int32), TOP_K)
    onehot = jax.nn.one_hot(flat_e, N_EXPERTS, dtype=jnp.int32)
    counts = jnp.sum(onehot, axis=0)
    rank = jnp.take_along_axis(jnp.cumsum(onehot, axis=0) - onehot, flat_e[:, None], axis=1)[:, 0]
    padded = (counts + MOE_TM - 1) // MOE_TM * MOE_TM
    ends = jnp.cumsum(padded)
    starts = ends - padded
    dest = (starts[flat_e] + rank).astype(jnp.int32)
    n_rows = t * TOP_K + N_EXPERTS * MOE_TM
    row_tok = jnp.zeros((n_rows,), jnp.int32).at[dest].set(flat_t)
    tile_start = jnp.arange(n_rows // MOE_TM, dtype=jnp.int32) * MOE_TM
    tile_e = jnp.minimum(jnp.sum(tile_start[:, None] >= ends[None, :], axis=1), N_EXPERTS - 1).astype(jnp.int32)
    n_used = (ends[-1:] // MOE_TM).astype(jnp.int32)
    return dest, row_tok, tile_e, n_used


def _moe_block(h, logits, x2, mod_l, ln_g, ln_b, w_in, b_in, w_out, b_out, layer, seq):
    top_idx, probs = _route_topk(logits)
    dest, row_tok, tile_e, n_used = _routing_tables(top_idx)
    xs = _moe_gather(h, row_tok)
    hid = _expert_ffn1(xs, tile_e, n_used, w_in, b_in, layer)
    y = _expert_ffn2(hid, tile_e, n_used, w_out, b_out, layer)
    return _combine_ln(y, dest, probs, x2, mod_l, ln_g, ln_b, seq)


def kernel(x, c, positions, cond_w, cond_b, ln_g, ln_b, attn_w_qkv, attn_w_o, sg_w_in, sg_b_in, sg_ln_g, sg_ln_b, sg_w_spatial, sg_b_spatial, sg_w_out, router_w, router_b, expert_w_in, expert_b_in, expert_w_out, expert_b_out):
    bsz, seq, d = x.shape
    x2 = x.reshape(bsz * seq, d)
    mod = _modulation(c, cond_w, cond_b).reshape(DEPTH, bsz, N_MODULATIONS, d)
    tabs = _rope_tables(positions)

    w_qkv = attn_w_qkv[0].astype(BF16).reshape(d, 3, len(DILATED_GROUPS), GROUP_WIDTH)
    outs, lses = [], []
    for g, (_, dilation) in enumerate(DILATED_GROUPS):
        w_g = w_qkv[:, :, g, :].reshape(d, 3 * GROUP_WIDTH)
        qkv = _qkv_projection(x2, mod[0], w_g, tabs, bsz, seq, dilation)
        o, lse = _dilated_attention(qkv, dilation)
        outs.append(o)
        lses.append(lse)
    mixed = _merge_groups(outs, lses, seq)
    x2, h, logits = _proj_ln(mixed, attn_w_o[0].astype(BF16), x2, mod[0], ln_g[0, 0], ln_b[0, 0],
                             router_w[0], router_b[0], seq)
    x2 = _moe_block(h, logits, x2, mod[0], ln_g[0, 1], ln_b[0, 1], expert_w_in, expert_b_in[0],
                    expert_w_out, expert_b_out[0], 0, seq)

    z = _sg_in_projection(x2, mod[1], sg_w_in[0].astype(BF16), sg_b_in[0], seq)
    gated = _spatial_gating(z, sg_ln_g[0], sg_ln_b[0], sg_w_spatial[0], sg_b_spatial[0])
    x2, h, logits = _proj_ln(gated, sg_w_out[0].astype(BF16), x2, mod[1], ln_g[1, 0], ln_b[1, 0],
                             router_w[1], router_b[1], seq)
    x2 = _moe_block(h, logits, x2, mod[1], ln_g[1, 1], ln_b[1, 1], expert_w_in, expert_b_in[1],
                    expert_w_out, expert_b_out[1], 1, seq)
    return x2.reshape(bsz, seq, d)
```

```python
import functools

import jax
import jax.numpy as jnp
from jax import lax
from jax.experimental import pallas as pl
from jax.experimental.pallas import tpu as pltpu

F32 = jnp.float32
BF16 = jnp.bfloat16

D_MODEL = 2048
DEPTH = 2
HEAD_DIM = 64
HEADS_PER_GROUP = 16
DILATED_GROUPS = ((128, 1), (512, 4), (2048, 16))
GROUP_WIDTH = HEADS_PER_GROUP * HEAD_DIM
ATT_WIDTH = GROUP_WIDTH * len(DILATED_GROUPS)
ROPE_THETA = 500000.0
ROT_DIM = HEAD_DIM // 4
ATT_BLOCK = 128

SG_CHUNK = 128
SG_WIDTH = 2 * D_MODEL
SG_GROUPS = 16
SG_GROUP_DIM = SG_WIDTH // SG_GROUPS

N_EXPERTS = 32
TOP_K = 4
EXPERT_FF = D_MODEL
SWIGLU_LIMIT = 7.0
SWIGLU_ALPHA = 1.702

N_MODULATIONS = 6
DEEPNORM_ALPHA = (2 * DEPTH) ** 0.25
LN_EPS = 1e-5
NEG_BIG = -1e30

LANES = 128
TOKEN_CHUNKS = D_MODEL // LANES
VMEM_LIMIT_BYTES = 56 * 1024 * 1024

MOD_TN = 512
ROPE_TM = 1024
PROJ_TM = 1024
PROJ_TN = 1024
MERGE_TM = 512
LN_TM = 256
TOPK_TM = 1024
SGU_TM = 256
MOE_TM = 256
WEIGHT_TN = 512
FF_PHASES = EXPERT_FF // WEIGHT_TN
OUT_PHASES = D_MODEL // (2 * WEIGHT_TN)
EXPERT_MAX_ROWS = 5 * MOE_TM
GATHER_ROWS = 512
COMBINE_TM = 256


def _cparams(sem):
    return pltpu.CompilerParams(dimension_semantics=sem, vmem_limit_bytes=VMEM_LIMIT_BYTES)


def _store_token_major(ref, val):
    rows = val.shape[0]
    for c in range(TOKEN_CHUNKS):
        ref[pl.ds(c, rows, stride=TOKEN_CHUNKS), :] = val[:, c * LANES:(c + 1) * LANES]


def _load_token_major(ref, rows):
    return jnp.concatenate(
        [ref[pl.ds(c, rows, stride=TOKEN_CHUNKS), :] for c in range(TOKEN_CHUNKS)], axis=1)


def _layer_norm(z, g, b):
    mu = jnp.mean(z, axis=-1, keepdims=True)
    zc = z - mu
    var = jnp.mean(zc * zc, axis=-1, keepdims=True)
    return zc * lax.rsqrt(var + LN_EPS) * g + b


def _mod_kernel(ct_ref, w_ref, b_ref, o_ref):
    ct = ct_ref[...]
    ca = ct * jax.nn.sigmoid(ct)
    w = w_ref[...]
    for b in range(ct.shape[1]):
        o_ref[b:b + 1, :] = jnp.sum(ca[:, b:b + 1] * w, axis=0, keepdims=True) + b_ref[...]


def _modulation(c, cond_w, cond_b):
    nl, d, n = cond_w.shape
    bsz = c.shape[0]
    return pl.pallas_call(
        _mod_kernel,
        grid=(nl, n // MOD_TN),
        in_specs=[
            pl.BlockSpec((d, bsz), lambda l, j: (0, 0)),
            pl.BlockSpec((None, d, MOD_TN), lambda l, j: (l, 0, j)),
            pl.BlockSpec((None, 1, MOD_TN), lambda l, j: (l, 0, j)),
        ],
        out_specs=pl.BlockSpec((None, bsz, MOD_TN), lambda l, j: (l, 0, j)),
        out_shape=jax.ShapeDtypeStruct((nl, bsz, n), F32),
        compiler_params=_cparams(("parallel", "parallel")),
        name="modulation",
    )(c.T, cond_w, cond_b.reshape(nl, 1, n))


def _rope_kernel(pos_ref, inv_ref, c_ref, s1_ref, s2_ref):
    ang = pos_ref[...] * inv_ref[...]
    lane = lax.broadcasted_iota(jnp.int32, ang.shape, 1) & (HEAD_DIM - 1)
    cs = jnp.cos(ang)
    sn = jnp.sin(ang)
    half = ROT_DIM // 2
    c_ref[...] = jnp.where(lane < ROT_DIM, cs, 1.0)
    s1_ref[...] = jnp.where(lane < half, -sn, 0.0)
    s2_ref[...] = jnp.where((lane >= half) & (lane < ROT_DIM), sn, 0.0)


def _rope_tables(positions):
    t = positions.size
    pos = positions.astype(F32).reshape(t, 1)
    inv = jnp.power(jnp.float32(ROPE_THETA), -jnp.arange(0, ROT_DIM, 2, dtype=F32) / ROT_DIM)
    lane = jnp.arange(LANES) % HEAD_DIM
    inv_row = inv[lane % (ROT_DIM // 2)].reshape(1, LANES)
    spec = pl.BlockSpec((ROPE_TM, LANES), lambda i: (i, 0))
    shp = jax.ShapeDtypeStruct((t, LANES), F32)
    return pl.pallas_call(
        _rope_kernel,
        grid=(t // ROPE_TM,),
        in_specs=[pl.BlockSpec((ROPE_TM, 1), lambda i: (i, 0)), pl.BlockSpec((1, LANES), lambda i: (0, 0))],
        out_specs=[spec, spec, spec],
        out_shape=[shp, shp, shp],
        compiler_params=_cparams(("parallel",)),
        name="rope_tables",
    )(pos, inv_row)


def _modulate_into(h_scr, x_ref, mod_ref):
    shift = mod_ref[0:1, :]
    scale = mod_ref[1:2, :]
    h_scr[...] = (x_ref[...] * (1.0 + scale) + shift).astype(BF16)


def _qkv_kernel(x_ref, mod_ref, w_ref, c_ref, s1_ref, s2_ref, o_ref, h_scr, r_scr, *, dilation):
    j = pl.program_id(1)

    @pl.when(j == 0)
    def _():
        _modulate_into(h_scr, x_ref, mod_ref)

    acc = jnp.dot(h_scr[...], w_ref[...], preferred_element_type=F32)
    n_col_blocks = acc.shape[1] // LANES

    @pl.when(j < 2)
    def _():
        c = c_ref[...]
        s1 = s1_ref[...]
        s2 = s2_ref[...]
        for cb in range(n_col_blocks):
            a = acc[:, cb * LANES:(cb + 1) * LANES]
            up = pltpu.roll(a, LANES - ROT_DIM // 2, 1)
            dn = pltpu.roll(a, ROT_DIM // 2, 1)
            r_scr[cb] = a * c + up * s1 + dn * s2

    @pl.when(j == 2)
    def _():
        for cb in range(n_col_blocks):
            r_scr[cb] = acc[:, cb * LANES:(cb + 1) * LANES]

    rows = r_scr.shape[1] // dilation
    for r in range(dilation):
        for cb in range(n_col_blocks):
            o_ref[r, :, cb * LANES:(cb + 1) * LANES] = (
                r_scr[cb, pl.ds(r, rows, stride=dilation), :].astype(BF16))


def _qkv_projection(x2, mod_l, w_bf, tabs, bsz, seq, dilation):
    t, d = x2.shape
    tiles_per_seq = seq // PROJ_TM
    tab_spec = pl.BlockSpec((PROJ_TM, LANES), lambda i, j: (i, 0))
    return pl.pallas_call(
        functools.partial(_qkv_kernel, dilation=dilation),
        grid=(t // PROJ_TM, 3),
        in_specs=[
            pl.BlockSpec((PROJ_TM, d), lambda i, j: (i, 0)),
            pl.BlockSpec((None, N_MODULATIONS, d), lambda i, j: (i // tiles_per_seq, 0, 0)),
            pl.BlockSpec((d, GROUP_WIDTH), lambda i, j: (0, j)),
            tab_spec, tab_spec, tab_spec,
        ],
        out_specs=pl.BlockSpec((None, dilation, PROJ_TM // dilation, GROUP_WIDTH),
                               lambda i, j: (i // tiles_per_seq, 0, i % tiles_per_seq, j)),
        out_shape=jax.ShapeDtypeStruct((bsz, dilation, seq // dilation, 3 * GROUP_WIDTH), BF16),
        scratch_shapes=[pltpu.VMEM((PROJ_TM, d), BF16),
                        pltpu.VMEM((GROUP_WIDTH // LANES, PROJ_TM, LANES), F32)],
        compiler_params=_cparams(("arbitrary", "arbitrary")),
        name=f"qkv_projection_d{dilation}",
    )(x2, mod_l, w_bf, *tabs)


def _sg_in_kernel(x_ref, mod_ref, w_ref, b_ref, o_ref, h_scr):
    @pl.when(pl.program_id(1) == 0)
    def _():
        _modulate_into(h_scr, x_ref, mod_ref)

    z = jnp.dot(h_scr[...], w_ref[...], preferred_element_type=F32) + b_ref[...]
    o_ref[...] = (0.5 * z * (1.0 + lax.erf(z * (2.0 ** -0.5)))).astype(BF16)


def _sg_in_projection(x2, mod_l, w_bf, b_in, seq):
    t, d = x2.shape
    n = w_bf.shape[1]
    tiles_per_seq = seq // PROJ_TM
    return pl.pallas_call(
        _sg_in_kernel,
        grid=(t // PROJ_TM, n // PROJ_TN),
        in_specs=[
            pl.BlockSpec((PROJ_TM, d), lambda i, j: (i, 0)),
            pl.BlockSpec((None, N_MODULATIONS, d), lambda i, j: (i // tiles_per_seq, 0, 0)),
            pl.BlockSpec((d, PROJ_TN), lambda i, j: (0, j)),
            pl.BlockSpec((1, PROJ_TN), lambda i, j: (0, j)),
        ],
        out_specs=pl.BlockSpec((PROJ_TM, PROJ_TN), lambda i, j: (i, j)),
        out_shape=jax.ShapeDtypeStruct((t, n), BF16),
        scratch_shapes=[pltpu.VMEM((PROJ_TM, d), BF16)],
        compiler_params=_cparams(("arbitrary", "arbitrary")),
        name="sg_in_projection",
    )(x2, mod_l, w_bf, b_in.reshape(1, n))


def _attn_kernel(q_ref, kp_ref, kc_ref, vp_ref, vc_ref, o_ref, lse_ref):
    n = pl.program_id(2)
    blk = q_ref.shape[0]
    row = lax.broadcasted_iota(jnp.int32, (blk, blk), 0)
    col = lax.broadcasted_iota(jnp.int32, (blk, blk), 1)
    prev_ok = (col >= row) & (n > 0)
    cur_ok = col <= row
    lane = lax.broadcasted_iota(jnp.int32, (1, LANES), 1)
    low = lane < HEAD_DIM
    keep = [jnp.where(low, 1.0, 0.0).astype(BF16), jnp.where(low, 0.0, 1.0).astype(BF16)]
    low_full = lax.broadcasted_iota(jnp.int32, (blk, LANES), 1) < HEAD_DIM
    dn = (((1,), (1,)), ((), ()))
    for pair in range(GROUP_WIDTH // LANES):
        cs = slice(pair * LANES, (pair + 1) * LANES)
        q = q_ref[:, cs] * (HEAD_DIM ** -0.5)
        kp, kc, vp, vc = kp_ref[:, cs], kc_ref[:, cs], vp_ref[:, cs], vc_ref[:, cs]
        o_half, lse_half = [], []
        for half in range(2):
            mine, other = keep[half], keep[1 - half]
            sp = lax.dot_general(q, kp * mine, dn, preferred_element_type=F32)
            sc = lax.dot_general(q, kc * mine, dn, preferred_element_type=F32)
            sp = jnp.where(prev_ok, sp, NEG_BIG)
            sc = jnp.where(cur_ok, sc, NEG_BIG)
            m = jnp.max(jnp.maximum(sp, sc), axis=-1, keepdims=True)
            pp = jnp.exp(sp - m).astype(BF16)
            pc = jnp.exp(sc - m).astype(BF16)
            acc = jnp.dot(pp, vp * mine + other, preferred_element_type=F32)
            acc = acc + jnp.dot(pc, vc * mine + other, preferred_element_type=F32)
            den = pltpu.roll(acc, HEAD_DIM, 1)
            o_half.append(acc / den)
            lse_half.append(m + jnp.log(den))
        o_ref[:, cs] = jnp.where(low_full, o_half[0], o_half[1]).astype(BF16)
        lse_ref[:, cs] = jnp.where(low_full, lse_half[0], lse_half[1])


def _dilated_attention(qkv, dilation):
    bsz, _, length, _ = qkv.shape
    nblk = length // ATT_BLOCK

    def spec(part, prev):
        def index(b, r, n):
            return (b, r, jnp.maximum(n - 1, 0) if prev else n, part)
        return pl.BlockSpec((None, None, ATT_BLOCK, GROUP_WIDTH), index)

    out_spec = pl.BlockSpec((None, None, ATT_BLOCK, GROUP_WIDTH), lambda b, r, n: (b, r, n, 0))
    return pl.pallas_call(
        _attn_kernel,
        grid=(bsz, dilation, nblk),
        in_specs=[spec(0, False), spec(1, True), spec(1, False), spec(2, True), spec(2, False)],
        out_specs=[out_spec, out_spec],
        out_shape=[jax.ShapeDtypeStruct((bsz, dilation, length, GROUP_WIDTH), BF16),
                   jax.ShapeDtypeStruct((bsz, dilation, length, GROUP_WIDTH), F32)],
        compiler_params=_cparams(("parallel", "parallel", "parallel")),
        name=f"dilated_attention_d{dilation}",
    )(qkv, qkv, qkv, qkv, qkv)


def _merge_kernel(*refs):
    ng = len(DILATED_GROUPS)
    o_refs, l_refs, out_ref = refs[:ng], refs[ng:2 * ng], refs[2 * ng]
    scratch = refs[2 * ng + 1:]
    outs, lses = [], []
    for g, (_, dilation) in enumerate(DILATED_GROUPS):
        if dilation == 1:
            outs.append(o_refs[g][0].astype(F32))
            lses.append(l_refs[g][0])
            continue
        so, sl = scratch[2 * (g - 1)], scratch[2 * (g - 1) + 1]
        n_col_blocks = so.shape[0]
        rows = so.shape[1] // dilation
        for r in range(dilation):
            for cb in range(n_col_blocks):
                cs = slice(cb * LANES, (cb + 1) * LANES)
                so[cb, pl.ds(r, rows, stride=dilation), :] = o_refs[g][r, :, cs].astype(F32)
                sl[cb, pl.ds(r, rows, stride=dilation), :] = l_refs[g][r, :, cs]
        outs.append(jnp.concatenate([so[cb] for cb in range(n_col_blocks)], axis=1))
        lses.append(jnp.concatenate([sl[cb] for cb in range(n_col_blocks)], axis=1))
    mx = jnp.maximum(jnp.maximum(lses[0], lses[1]), lses[2])
    es = [jnp.exp(l - mx) for l in lses]
    inv = 1.0 / (es[0] + es[1] + es[2])
    for g in range(ng):
        out_ref[:, g * GROUP_WIDTH:(g + 1) * GROUP_WIDTH] = (outs[g] * (es[g] * inv)).astype(BF16)


def _merge_groups(outs, lses, seq):
    bsz = outs[0].shape[0]
    tiles_per_seq = seq // MERGE_TM
    specs = [pl.BlockSpec((None, dilation, MERGE_TM // dilation, GROUP_WIDTH),
                          lambda i: (i // tiles_per_seq, 0, i % tiles_per_seq, 0))
             for _, dilation in DILATED_GROUPS]
    assert DILATED_GROUPS[0][1] == 1
    scratch = []
    for _ in DILATED_GROUPS[1:]:
        scratch += [pltpu.VMEM((GROUP_WIDTH // LANES, MERGE_TM, LANES), F32)] * 2
    return pl.pallas_call(
        _merge_kernel,
        grid=(bsz * tiles_per_seq,),
        in_specs=specs + specs,
        out_specs=pl.BlockSpec((MERGE_TM, ATT_WIDTH), lambda i: (i, 0)),
        out_shape=jax.ShapeDtypeStruct((bsz * seq, ATT_WIDTH), BF16),
        scratch_shapes=scratch,
        compiler_params=_cparams(("parallel",)),
        name="merge_groups",
    )(*outs, *lses)


def _sgu_kernel(u_ref, v_ref, g_ref, b_ref, wsp_ref, bsp_ref, o_ref):
    v = v_ref[...].astype(F32)
    vn = _layer_norm(v, g_ref[...], b_ref[...]).astype(BF16)
    row = lax.broadcasted_iota(jnp.int32, (SG_CHUNK, SG_CHUNK), 0)
    col = lax.broadcasted_iota(jnp.int32, (SG_CHUNK, SG_CHUNK), 1)
    causal = col <= row
    for g in range(SG_GROUPS):
        w = jnp.where(causal, wsp_ref[g], 0.0).astype(BF16)
        bias = bsp_ref[:, g:g + 1]
        cs = slice(g * SG_GROUP_DIM, (g + 1) * SG_GROUP_DIM)
        for c in range(u_ref.shape[0] // SG_CHUNK):
            rs = slice(c * SG_CHUNK, (c + 1) * SG_CHUNK)
            mixed = jnp.dot(w, vn[rs, cs], preferred_element_type=F32) + bias
            o_ref[rs, cs] = (u_ref[rs, cs].astype(F32) * mixed).astype(BF16)


def _spatial_gating(z, ln_g, ln_b, w_sp, b_sp):
    t = z.shape[0]
    return pl.pallas_call(
        _sgu_kernel,
        grid=(t // SGU_TM,),
        in_specs=[
            pl.BlockSpec((SGU_TM, SG_WIDTH), lambda i: (i, 0)),
            pl.BlockSpec((SGU_TM, SG_WIDTH), lambda i: (i, 1)),
            pl.BlockSpec((1, SG_WIDTH), lambda i: (0, 0)),
            pl.BlockSpec((1, SG_WIDTH), lambda i: (0, 0)),
            pl.BlockSpec((SG_GROUPS, SG_CHUNK, SG_CHUNK), lambda i: (0, 0, 0)),
            pl.BlockSpec((SG_CHUNK, SG_GROUPS), lambda i: (0, 0)),
        ],
        out_specs=pl.BlockSpec((SGU_TM, SG_WIDTH), lambda i: (i, 0)),
        out_shape=jax.ShapeDtypeStruct((t, SG_WIDTH), BF16),
        compiler_params=_cparams(("parallel",)),
        name="spatial_gating",
    )(z, z, ln_g.reshape(1, -1), ln_b.reshape(1, -1), w_sp, b_sp.T)


def _proj_ln_kernel(a_ref, w_ref, x_ref, mod_ref, g_ref, b_ref, rw_hi_ref, rw_lo_ref, rb_ref,
                    x_out, h_out, logit_out):
    y = jnp.dot(a_ref[...], w_ref[...], preferred_element_type=F32)
    gate = mod_ref[2:3, :]
    xn = _layer_norm(DEEPNORM_ALPHA * x_ref[...] + (1.0 + gate) * y, g_ref[...], b_ref[...])
    x_out[...] = xn
    h = xn * (1.0 + mod_ref[4:5, :]) + mod_ref[3:4, :]
    _store_token_major(h_out, h)
    h_hi = h.astype(BF16)
    h_lo = (h - h_hi.astype(F32)).astype(BF16)
    rw_hi = rw_hi_ref[...]
    logits = jnp.dot(h_hi, rw_hi, preferred_element_type=F32)
    logits = logits + jnp.dot(h_hi, rw_lo_ref[...], preferred_element_type=F32)
    logits = logits + jnp.dot(h_lo, rw_hi, preferred_element_type=F32)
    logit_out[...] = logits + rb_ref[...]


def _proj_ln(a, w_bf, x2, mod_l, ln_g, ln_b, router_w, router_b, seq):
    t, d = x2.shape
    k = a.shape[1]
    tiles_per_seq = seq // LN_TM
    rw_hi = router_w.astype(BF16)
    rw_lo = (router_w - rw_hi.astype(F32)).astype(BF16)
    row = lambda i: (i, 0)
    fixed = lambda i: (0, 0)
    return pl.pallas_call(
        _proj_ln_kernel,
        grid=(t // LN_TM,),
        in_specs=[
            pl.BlockSpec((LN_TM, k), row),
            pl.BlockSpec((k, d), fixed, pipeline_mode=pl.Buffered(1)),
            pl.BlockSpec((LN_TM, d), row),
            pl.BlockSpec((None, N_MODULATIONS, d), lambda i: (i // tiles_per_seq, 0, 0)),
            pl.BlockSpec((1, d), fixed),
            pl.BlockSpec((1, d), fixed),
            pl.BlockSpec((d, N_EXPERTS), fixed),
            pl.BlockSpec((d, N_EXPERTS), fixed),
            pl.BlockSpec((1, N_EXPERTS), fixed),
        ],
        out_specs=[pl.BlockSpec((LN_TM, d), row), pl.BlockSpec((LN_TM * TOKEN_CHUNKS, LANES), row),
                   pl.BlockSpec((LN_TM, N_EXPERTS), row)],
        out_shape=[jax.ShapeDtypeStruct((t, d), F32), jax.ShapeDtypeStruct((t * TOKEN_CHUNKS, LANES), F32),
                   jax.ShapeDtypeStruct((t, N_EXPERTS), F32)],
        compiler_params=_cparams(("parallel",)),
        name="proj_ln",
    )(a, w_bf, x2, mod_l, ln_g.reshape(1, d), ln_b.reshape(1, d), rw_hi, rw_lo,
      router_b.reshape(1, N_EXPERTS))


def _topk_kernel(l_ref, idx_ref, p_ref):
    l = l_ref[...]
    tm = l.shape[0]
    lane = lax.broadcasted_iota(jnp.int32, l.shape, 1)
    out_lane = lax.broadcasted_iota(jnp.int32, (tm, TOP_K), 1)
    idx_out = jnp.zeros((tm, TOP_K), jnp.int32)
    val_out = jnp.zeros((tm, TOP_K), F32)
    for k in range(TOP_K):
        m = jnp.max(l, axis=-1, keepdims=True)
        idx = jnp.min(jnp.where(l == m, lane, N_EXPERTS), axis=-1, keepdims=True)
        idx_out = jnp.where(out_lane == k, idx, idx_out)
        val_out = jnp.where(out_lane == k, m, val_out)
        l = jnp.where(lane == idx, -jnp.inf, l)
    e = jnp.exp(val_out - jnp.max(val_out, axis=-1, keepdims=True))
    idx_ref[...] = idx_out
    p_ref[...] = e / jnp.sum(e, axis=-1, keepdims=True)


def _route_topk(logits):
    t = logits.shape[0]
    return pl.pallas_call(
        _topk_kernel,
        grid=(t // TOPK_TM,),
        in_specs=[pl.BlockSpec((TOPK_TM, N_EXPERTS), lambda i: (i, 0))],
        out_specs=[pl.BlockSpec((TOPK_TM, TOP_K), lambda i: (i, 0))] * 2,
        out_shape=[jax.ShapeDtypeStruct((t, TOP_K), jnp.int32), jax.ShapeDtypeStruct((t, TOP_K), F32)],
        compiler_params=_cparams(("parallel",)),
        name="route_topk",
    )(logits)


def _slab_copy(src_hbm, dst_vmem, sem, src_tok, dst_tok):
    src = pl.multiple_of(src_tok * TOKEN_CHUNKS, TOKEN_CHUNKS)
    dst = pl.multiple_of(dst_tok * TOKEN_CHUNKS, TOKEN_CHUNKS)
    return pltpu.make_async_copy(src_hbm.at[pl.ds(src, TOKEN_CHUNKS)], dst_vmem.at[pl.ds(dst, TOKEN_CHUNKS)], sem)


def _gather_kernel(tok_ref, h_hbm, o_ref, buf, sem):
    i = pl.program_id(0)
    n_steps = pl.num_programs(0)

    def start_all(step, slot):
        def body(r2, carry):
            for prio in range(2):
                r = 2 * r2 + prio
                _slab_copy(h_hbm, buf.at[slot], sem.at[slot], tok_ref[step * GATHER_ROWS + r], r).start(
                    priority=prio)
            return carry
        lax.fori_loop(0, GATHER_ROWS // 2, body, 0, unroll=4)

    def wait_all(slot):
        def body(r, carry):
            _slab_copy(h_hbm, buf.at[slot], sem.at[slot], 0, r).wait()
            return carry
        lax.fori_loop(0, GATHER_ROWS, body, 0, unroll=8)

    @pl.when(i == 0)
    def _():
        start_all(0, 0)

    for slot in range(2):
        @pl.when((i % 2 == slot) & (i + 1 < n_steps))
        def _():
            start_all(i + 1, 1 - slot)

        @pl.when(i % 2 == slot)
        def _():
            wait_all(slot)
            o_ref[...] = _load_token_major(buf.at[slot], GATHER_ROWS).astype(BF16)


def _moe_gather(h_tm, row_tok):
    n_rows = row_tok.shape[0]
    return pl.pallas_call(
        _gather_kernel,
        grid_spec=pltpu.PrefetchScalarGridSpec(
            num_scalar_prefetch=1,
            grid=(n_rows // GATHER_ROWS,),
            in_specs=[pl.BlockSpec(memory_space=pl.ANY)],
            out_specs=pl.BlockSpec((GATHER_ROWS, D_MODEL), lambda i, tok: (i, 0)),
            scratch_shapes=[pltpu.VMEM((2, GATHER_ROWS * TOKEN_CHUNKS, LANES), F32),
                            pltpu.SemaphoreType.DMA((2,))],
        ),
        out_shape=jax.ShapeDtypeStruct((n_rows, D_MODEL), BF16),
        compiler_params=_cparams(("arbitrary",)),
        name="moe_gather",
    )(row_tok, h_tm)


def _experts_kernel(we_ref, ws_ref, wn_ref, nw_ref, tail_ref, xs_hbm, win_hbm, wout_hbm, bin_ref, bout_ref, y_hbm,
                    wbuf, wbf, xbuf, hid, ystash, ybuf, wsem, xsem, ysem, *, layer):
    w = pl.program_id(0)
    p = pl.program_id(1)
    n_phase = FF_PHASES + OUT_PHASES
    n_work = nw_ref[0]
    active = w < n_work
    slot = (w * n_phase + p) % 2
    sub_rows = MOE_TM

    def weight_copy(src_hbm, expert, col_tile, sl, half):
        col = pl.multiple_of(col_tile * WEIGHT_TN, WEIGHT_TN)
        return pltpu.make_async_copy(src_hbm.at[layer, expert, :, pl.ds(col, WEIGHT_TN)],
                                     wbuf.at[sl, half], wsem.at[sl, half])

    def start_weights(item, phase, sl):
        expert = we_ref[item]

        @pl.when(phase < FF_PHASES)
        def _():
            weight_copy(win_hbm, expert, phase, sl, 0).start()
            weight_copy(win_hbm, expert, FF_PHASES + phase, sl, 1).start()

        @pl.when(phase >= FF_PHASES)
        def _():
            half_idx = phase - FF_PHASES
            weight_copy(wout_hbm, expert, 2 * half_idx, sl, 0).start()
            weight_copy(wout_hbm, expert, 2 * half_idx + 1, sl, 1).start()

    def x_copy(item, sub, xsl):
        row = pl.multiple_of(ws_ref[item] + sub * sub_rows, sub_rows)
        return pltpu.make_async_copy(xs_hbm.at[pl.ds(row, sub_rows)],
                                     xbuf.at[xsl, pl.ds(sub * sub_rows, sub_rows)], xsem.at[xsl])

    def for_each_x_copy(item, xsl, fn):
        for sub in range(EXPERT_MAX_ROWS // sub_rows):
            @pl.when(sub < wn_ref[item])
            def _():
                fn(x_copy(item, sub, xsl))

    def y_copy(item, sub, ysl):
        row = pl.multiple_of((ws_ref[item] + sub * sub_rows) * TOKEN_CHUNKS, sub_rows * TOKEN_CHUNKS)
        return pltpu.make_async_copy(ybuf.at[ysl], y_hbm.at[pl.ds(row, sub_rows * TOKEN_CHUNKS)], ysem.at[ysl])

    @pl.when((w == 0) & (p == 0))
    def _():
        start_weights(0, 0, 0)
        for_each_x_copy(0, 0, lambda cp: cp.start())
        ybuf[0] = jnp.zeros(ybuf.shape[1:], F32)

        def tail_copy(i):
            row = pl.multiple_of((tail_ref[0] + i) * (sub_rows * TOKEN_CHUNKS), sub_rows * TOKEN_CHUNKS)
            return pltpu.make_async_copy(ybuf.at[0], y_hbm.at[pl.ds(row, sub_rows * TOKEN_CHUNKS)], ysem.at[0])

        def start_tail(i, carry):
            tail_copy(i).start()
            return carry

        def wait_tail(i, carry):
            tail_copy(i).wait()
            return carry

        lax.fori_loop(0, tail_ref[1], start_tail, 0)
        lax.fori_loop(0, tail_ref[1], wait_tail, 0)

    last = p == n_phase - 1
    nxt_item = jnp.where(last, w + 1, w)
    nxt_phase = jnp.where(last, 0, p + 1)

    @pl.when(active & (nxt_item < n_work))
    def _():
        start_weights(nxt_item, nxt_phase, 1 - slot)

    @pl.when(active & (p == 0))
    def _():
        @pl.when(w + 1 < n_work)
        def _():
            for_each_x_copy(w + 1, (w + 1) % 2, lambda cp: cp.start())

        for_each_x_copy(w, w % 2, lambda cp: cp.wait())

    @pl.when(active)
    def _():
        for half in range(2):
            weight_copy(win_hbm, 0, 0, slot, half).wait()
            wbf[:, half * WEIGHT_TN:(half + 1) * WEIGHT_TN] = wbuf[slot, half].astype(BF16)
        n_sub = wn_ref[w]
        xsl = w % 2

        @pl.when(p < FF_PHASES)
        def _():
            bg = bin_ref[pl.ds(p, 1), :]
            bu = bin_ref[pl.ds(FF_PHASES + p, 1), :]

            def body(sub, carry):
                r0 = pl.multiple_of(sub * sub_rows, sub_rows)
                gu = jnp.dot(xbuf[xsl, pl.ds(r0, sub_rows), :], wbf[...], preferred_element_type=F32)
                gate = jnp.minimum(gu[:, :WEIGHT_TN] + bg, SWIGLU_LIMIT)
                up = jnp.clip(gu[:, WEIGHT_TN:] + bu, -SWIGLU_LIMIT, SWIGLU_LIMIT)
                hid[p, pl.ds(r0, sub_rows), :] = (
                    (up + 1.0) * gate * jax.nn.sigmoid(SWIGLU_ALPHA * gate)).astype(BF16)
                return carry

            lax.fori_loop(0, n_sub, body, 0)

        def out_half(sub, bo):
            r0 = pl.multiple_of(sub * sub_rows, sub_rows)
            acc = bo
            for f in range(FF_PHASES):
                acc = acc + jnp.dot(hid[f, pl.ds(r0, sub_rows), :], wbf[f * WEIGHT_TN:(f + 1) * WEIGHT_TN, :],
                                    preferred_element_type=F32)
            return r0, acc

        @pl.when(p == FF_PHASES)
        def _():
            bo = bout_ref[0:1, :]

            def body(sub, carry):
                r0, acc = out_half(sub, bo)
                ystash[pl.ds(r0, sub_rows), :] = acc
                return carry

            lax.fori_loop(0, n_sub, body, 0)

        @pl.when(p == FF_PHASES + 1)
        def _():
            bo = bout_ref[1:2, :]
            half_chunks = TOKEN_CHUNKS // 2

            def body(sub, carry):
                r0, acc = out_half(sub, bo)
                ysl = sub % 2

                @pl.when(sub >= 2)
                def _():
                    y_copy(w, sub, ysl).wait()

                stage = ybuf.at[ysl]
                for c in range(half_chunks):
                    cs = slice(c * LANES, (c + 1) * LANES)
                    stage[pl.ds(c, sub_rows, stride=TOKEN_CHUNKS), :] = ystash[pl.ds(r0, sub_rows), cs]
                    stage[pl.ds(half_chunks + c, sub_rows, stride=TOKEN_CHUNKS), :] = acc[:, cs]
                y_copy(w, sub, ysl).start()
                return carry

            lax.fori_loop(0, n_sub, body, 0)
            for back in (1, 2):
                @pl.when(n_sub >= back)
                def _():
                    y_copy(w, n_sub - back, (n_sub - back) % 2).wait()


def _expert_ffn(xs, work, w_in, b_in, w_out, b_out, layer):
    n_rows, d = xs.shape
    ne = w_in.shape[1]
    w_expert, w_start, w_nsub, n_work, tail = work
    n_items = w_expert.shape[0]
    bias_in = pl.BlockSpec((None, 2 * FF_PHASES, WEIGHT_TN), lambda w, p, we, *_: (we[w], 0, 0))
    bias_out = pl.BlockSpec((None, OUT_PHASES, 2 * WEIGHT_TN), lambda w, p, we, *_: (we[w], 0, 0))
    any_spec = pl.BlockSpec(memory_space=pl.ANY)
    return pl.pallas_call(
        functools.partial(_experts_kernel, layer=layer),
        grid_spec=pltpu.PrefetchScalarGridSpec(
            num_scalar_prefetch=5,
            grid=(n_items, FF_PHASES + OUT_PHASES),
            in_specs=[any_spec, any_spec, any_spec, bias_in, bias_out],
            out_specs=any_spec,
            scratch_shapes=[
                pltpu.VMEM((2, 2, d, WEIGHT_TN), F32),
                pltpu.VMEM((d, 2 * WEIGHT_TN), BF16),
                pltpu.VMEM((2, EXPERT_MAX_ROWS, d), BF16),
                pltpu.VMEM((FF_PHASES, EXPERT_MAX_ROWS, WEIGHT_TN), BF16),
                pltpu.VMEM((EXPERT_MAX_ROWS, 2 * WEIGHT_TN), F32),
                pltpu.VMEM((2, MOE_TM * TOKEN_CHUNKS, LANES), F32),
                pltpu.SemaphoreType.DMA((2, 2)),
                pltpu.SemaphoreType.DMA((2,)),
                pltpu.SemaphoreType.DMA((2,)),
            ],
        ),
        out_shape=jax.ShapeDtypeStruct((n_rows * TOKEN_CHUNKS, LANES), F32),
        compiler_params=_cparams(("arbitrary", "arbitrary")),
        name="expert_ffn",
    )(w_expert, w_start, w_nsub, n_work, tail, xs, w_in, w_out,
      b_in.reshape(ne, 2 * FF_PHASES, WEIGHT_TN), b_out.reshape(ne, OUT_PHASES, 2 * WEIGHT_TN))


def _combine_kernel(dest_ref, y_hbm, p_ref, x_ref, mod_ref, g_ref, b_ref, o_ref, buf, sem):
    i = pl.program_id(0)
    n_steps = pl.num_programs(0)

    def start_all(step, slot):
        def body(r, carry):
            for k in range(TOP_K):
                row = dest_ref[(step * COMBINE_TM + r) * TOP_K + k]
                _slab_copy(y_hbm, buf.at[slot, k], sem.at[slot], row, r).start(priority=k % 2)
            return carry
        lax.fori_loop(0, COMBINE_TM, body, 0, unroll=2)

    def wait_all(slot):
        def body(r, carry):
            for k in range(TOP_K):
                _slab_copy(y_hbm, buf.at[slot, k], sem.at[slot], 0, r).wait()
            return carry
        lax.fori_loop(0, COMBINE_TM, body, 0, unroll=2)

    @pl.when(i == 0)
    def _():
        start_all(0, 0)

    for slot in range(2):
        @pl.when((i % 2 == slot) & (i + 1 < n_steps))
        def _():
            start_all(i + 1, 1 - slot)

        @pl.when(i % 2 == slot)
        def _():
            wait_all(slot)
            p = p_ref[...]
            moe = p[:, 0:1] * _load_token_major(buf.at[slot, 0], COMBINE_TM)
            for k in range(1, TOP_K):
                moe = moe + p[:, k:k + 1] * _load_token_major(buf.at[slot, k], COMBINE_TM)
            gate = mod_ref[5:6, :]
            o_ref[...] = _layer_norm(DEEPNORM_ALPHA * x_ref[...] + (1.0 + gate) * moe, g_ref[...], b_ref[...])


def _combine_ln(y, dest, probs, x2, mod_l, ln_g, ln_b, seq):
    t, d = x2.shape
    tiles_per_seq = seq // COMBINE_TM
    return pl.pallas_call(
        _combine_kernel,
        grid_spec=pltpu.PrefetchScalarGridSpec(
            num_scalar_prefetch=1,
            grid=(t // COMBINE_TM,),
            in_specs=[
                pl.BlockSpec(memory_space=pl.ANY),
                pl.BlockSpec((COMBINE_TM, TOP_K), lambda i, dst: (i, 0)),
                pl.BlockSpec((COMBINE_TM, d), lambda i, dst: (i, 0)),
                pl.BlockSpec((None, N_MODULATIONS, d), lambda i, dst: (i // tiles_per_seq, 0, 0)),
                pl.BlockSpec((1, d), lambda i, dst: (0, 0)),
                pl.BlockSpec((1, d), lambda i, dst: (0, 0)),
            ],
            out_specs=pl.BlockSpec((COMBINE_TM, d), lambda i, dst: (i, 0)),
            scratch_shapes=[pltpu.VMEM((2, TOP_K, COMBINE_TM * TOKEN_CHUNKS, LANES), F32),
                            pltpu.SemaphoreType.DMA((2,))],
        ),
        out_shape=jax.ShapeDtypeStruct((t, d), F32),
        compiler_params=_cparams(("arbitrary",)),
        name="combine_ln",
    )(dest, y, probs, x2, mod_l, ln_g.reshape(1, d), ln_b.reshape(1, d))


def _routing_tables(top_idx):
    t = top_idx.shape[0]
    flat_e = top_idx.reshape(-1)
    flat_t = jnp.repeat(jnp.arange(t, dtype=jnp.int32), TOP_K)
    onehot = jax.nn.one_hot(flat_e, N_EXPERTS, dtype=jnp.int32)
    counts = jnp.sum(onehot, axis=0)
    rank = jnp.take_along_axis(jnp.cumsum(onehot, axis=0) - onehot, flat_e[:, None], axis=1)[:, 0]
    padded = (counts + MOE_TM - 1) // MOE_TM * MOE_TM
    ends = jnp.cumsum(padded)
    starts = ends - padded
    dest = (starts[flat_e] + rank).astype(jnp.int32)
    n_rows = t * TOP_K + N_EXPERTS * MOE_TM
    row_tok = jnp.zeros((n_rows,), jnp.int32).at[dest].set(flat_t)
    chunks = (padded + EXPERT_MAX_ROWS - 1) // EXPERT_MAX_ROWS
    chunk_ends = jnp.cumsum(chunks)
    n_work = chunk_ends[-1:]
    item = jnp.arange(N_EXPERTS + n_rows // EXPERT_MAX_ROWS, dtype=jnp.int32)
    w_expert = jnp.minimum(jnp.sum(item[:, None] >= chunk_ends[None, :], axis=1), N_EXPERTS - 1)
    w_chunk = item - (chunk_ends[w_expert] - chunks[w_expert])
    live = item < n_work
    w_start = jnp.where(live, starts[w_expert] + w_chunk * EXPERT_MAX_ROWS, 0)
    w_rows = jnp.clip(padded[w_expert] - w_chunk * EXPERT_MAX_ROWS, 0, EXPERT_MAX_ROWS)
    w_nsub = jnp.where(live, w_rows // MOE_TM, 0)
    used_tiles = ends[-1] // MOE_TM
    tail = jnp.stack([used_tiles, n_rows // MOE_TM - used_tiles])
    work = tuple(a.astype(jnp.int32) for a in (w_expert, w_start, w_nsub, n_work, tail))
    return dest, row_tok, work


def _moe_block(h, logits, x2, mod_l, ln_g, ln_b, w_in, b_in, w_out, b_out, layer, seq):
    top_idx, probs = _route_topk(logits)
    dest, row_tok, work = _routing_tables(top_idx)
    xs = _moe_gather(h, row_tok)
    y = _expert_ffn(xs, work, w_in, b_in, w_out, b_out, layer)
    return _combine_ln(y, dest, probs, x2, mod_l, ln_g, ln_b, seq)


def kernel(x, c, positions, cond_w, cond_b, ln_g, ln_b, attn_w_qkv, attn_w_o, sg_w_in, sg_b_in, sg_ln_g, sg_ln_b, sg_w_spatial, sg_b_spatial, sg_w_out, router_w, router_b, expert_w_in, expert_b_in, expert_w_out, expert_b_out):
    bsz, seq, d = x.shape
    x2 = x.reshape(bsz * seq, d)
    mod = _modulation(c, cond_w, cond_b).reshape(DEPTH, bsz, N_MODULATIONS, d)
    tabs = _rope_tables(positions)

    w_qkv = attn_w_qkv[0].astype(BF16).reshape(d, 3, len(DILATED_GROUPS), GROUP_WIDTH)
    outs, lses = [], []
    for g, (_, dilation) in enumerate(DILATED_GROUPS):
        w_g = w_qkv[:, :, g, :].reshape(d, 3 * GROUP_WIDTH)
        qkv = _qkv_projection(x2, mod[0], w_g, tabs, bsz, seq, dilation)
        o, lse = _dilated_attention(qkv, dilation)
        outs.append(o)
        lses.append(lse)
    mixed = _merge_groups(outs, lses, seq)
    x2, h, logits = _proj_ln(mixed, attn_w_o[0].astype(BF16), x2, mod[0], ln_g[0, 0], ln_b[0, 0],
                             router_w[0], router_b[0], seq)
    x2 = _moe_block(h, logits, x2, mod[0], ln_g[0, 1], ln_b[0, 1], expert_w_in, expert_b_in[0],
                    expert_w_out, expert_b_out[0], 0, seq)

    z = _sg_in_projection(x2, mod[1], sg_w_in[0].astype(BF16), sg_b_in[0], seq)
    gated = _spatial_gating(z, sg_ln_g[0], sg_ln_b[0], sg_w_spatial[0], sg_b_spatial[0])
    x2, h, logits = _proj_ln(gated, sg_w_out[0].astype(BF16), x2, mod[1], ln_g[1, 0], ln_b[1, 0],
                             router_w[1], router_b[1], seq)
    x2 = _moe_block(h, logits, x2, mod[1], ln_g[1, 1], ln_b[1, 1], expert_w_in, expert_b_in[1],
                    expert_w_out, expert_b_out[1], 1, seq)
    return x2.reshape(bsz, seq, d)
```

```python
import functools

import jax
import jax.numpy as jnp
from jax import lax
from jax.experimental import pallas as pl
from jax.experimental.pallas import tpu as pltpu

F32 = jnp.float32
BF16 = jnp.bfloat16

D_MODEL = 2048
DEPTH = 2
HEAD_DIM = 64
HEADS_PER_GROUP = 16
DILATED_GROUPS = ((128, 1), (512, 4), (2048, 16))
GROUP_WIDTH = HEADS_PER_GROUP * HEAD_DIM
ATT_WIDTH = GROUP_WIDTH * len(DILATED_GROUPS)
ROPE_THETA = 500000.0
ROT_DIM = HEAD_DIM // 4
ATT_BLOCK = 128

SG_CHUNK = 128
SG_WIDTH = 2 * D_MODEL
SG_GROUPS = 16
SG_GROUP_DIM = SG_WIDTH // SG_GROUPS

N_EXPERTS = 32
TOP_K = 4
EXPERT_FF = D_MODEL
SWIGLU_LIMIT = 7.0
SWIGLU_ALPHA = 1.702

N_MODULATIONS = 6
DEEPNORM_ALPHA = (2 * DEPTH) ** 0.25
LN_EPS = 1e-5
NEG_BIG = -1e30

LANES = 128
TOKEN_CHUNKS = D_MODEL // LANES
SLAB_PITCH = TOKEN_CHUNKS + 1
VMEM_LIMIT_BYTES = 56 * 1024 * 1024

MOD_TN = 512
ROPE_TM = 1024
PROJ_TM = 1024
PROJ_TN = 1024
MERGE_TM = 512
LN_TM = 256
TOPK_TM = 1024
SGU_TM = 256
MOE_TM = 256
WEIGHT_TN = 512
FF_PHASES = EXPERT_FF // WEIGHT_TN
OUT_PHASES = D_MODEL // (2 * WEIGHT_TN)
EXPERT_MAX_ROWS = 5 * MOE_TM
GATHER_ROWS = 512
COMBINE_TM = 256


def _cparams(sem):
    return pltpu.CompilerParams(dimension_semantics=sem, vmem_limit_bytes=VMEM_LIMIT_BYTES)


def _store_token_major(ref, val):
    rows = val.shape[0]
    for c in range(TOKEN_CHUNKS):
        ref[pl.ds(c, rows, stride=TOKEN_CHUNKS), :] = val[:, c * LANES:(c + 1) * LANES]


def _load_token_major(ref, rows, pitch=TOKEN_CHUNKS):
    return jnp.concatenate(
        [ref[pl.ds(c, rows, stride=pitch), :] for c in range(TOKEN_CHUNKS)], axis=1)


def _layer_norm(z, g, b):
    mu = jnp.mean(z, axis=-1, keepdims=True)
    zc = z - mu
    var = jnp.mean(zc * zc, axis=-1, keepdims=True)
    return zc * lax.rsqrt(var + LN_EPS) * g + b


def _mod_kernel(ct_ref, w_ref, b_ref, o_ref):
    ct = ct_ref[...]
    ca = ct * jax.nn.sigmoid(ct)
    w = w_ref[...]
    for b in range(ct.shape[1]):
        o_ref[b:b + 1, :] = jnp.sum(ca[:, b:b + 1] * w, axis=0, keepdims=True) + b_ref[...]


def _modulation(c, cond_w, cond_b):
    nl, d, n = cond_w.shape
    bsz = c.shape[0]
    return pl.pallas_call(
        _mod_kernel,
        grid=(nl, n // MOD_TN),
        in_specs=[
            pl.BlockSpec((d, bsz), lambda l, j: (0, 0)),
            pl.BlockSpec((None, d, MOD_TN), lambda l, j: (l, 0, j)),
            pl.BlockSpec((None, 1, MOD_TN), lambda l, j: (l, 0, j)),
        ],
        out_specs=pl.BlockSpec((None, bsz, MOD_TN), lambda l, j: (l, 0, j)),
        out_shape=jax.ShapeDtypeStruct((nl, bsz, n), F32),
        compiler_params=_cparams(("parallel", "parallel")),
        name="modulation",
    )(c.T, cond_w, cond_b.reshape(nl, 1, n))


def _rope_kernel(pos_ref, inv_ref, c_ref, s1_ref, s2_ref):
    ang = pos_ref[...] * inv_ref[...]
    lane = lax.broadcasted_iota(jnp.int32, ang.shape, 1) & (HEAD_DIM - 1)
    cs = jnp.cos(ang)
    sn = jnp.sin(ang)
    half = ROT_DIM // 2
    c_ref[...] = jnp.where(lane < ROT_DIM, cs, 1.0)
    s1_ref[...] = jnp.where(lane < half, -sn, 0.0)
    s2_ref[...] = jnp.where((lane >= half) & (lane < ROT_DIM), sn, 0.0)


def _rope_tables(positions):
    t = positions.size
    pos = positions.astype(F32).reshape(t, 1)
    inv = jnp.power(jnp.float32(ROPE_THETA), -jnp.arange(0, ROT_DIM, 2, dtype=F32) / ROT_DIM)
    lane = jnp.arange(LANES) % HEAD_DIM
    inv_row = inv[lane % (ROT_DIM // 2)].reshape(1, LANES)
    spec = pl.BlockSpec((ROPE_TM, LANES), lambda i: (i, 0))
    shp = jax.ShapeDtypeStruct((t, LANES), F32)
    return pl.pallas_call(
        _rope_kernel,
        grid=(t // ROPE_TM,),
        in_specs=[pl.BlockSpec((ROPE_TM, 1), lambda i: (i, 0)), pl.BlockSpec((1, LANES), lambda i: (0, 0))],
        out_specs=[spec, spec, spec],
        out_shape=[shp, shp, shp],
        compiler_params=_cparams(("parallel",)),
        name="rope_tables",
    )(pos, inv_row)


def _modulate_into(h_scr, x_ref, mod_ref):
    shift = mod_ref[0:1, :]
    scale = mod_ref[1:2, :]
    h_scr[...] = (x_ref[...] * (1.0 + scale) + shift).astype(BF16)


def _qkv_kernel(x_ref, mod_ref, w_ref, c_ref, s1_ref, s2_ref, o_ref, h_scr, r_scr, *, dilation):
    j = pl.program_id(1)

    @pl.when(j == 0)
    def _():
        _modulate_into(h_scr, x_ref, mod_ref)

    acc = jnp.dot(h_scr[...], w_ref[...], preferred_element_type=F32)
    n_col_blocks = acc.shape[1] // LANES

    @pl.when(j < 2)
    def _():
        c = c_ref[...]
        s1 = s1_ref[...]
        s2 = s2_ref[...]
        for cb in range(n_col_blocks):
            a = acc[:, cb * LANES:(cb + 1) * LANES]
            up = pltpu.roll(a, LANES - ROT_DIM // 2, 1)
            dn = pltpu.roll(a, ROT_DIM // 2, 1)
            r_scr[cb] = a * c + up * s1 + dn * s2

    @pl.when(j == 2)
    def _():
        for cb in range(n_col_blocks):
            r_scr[cb] = acc[:, cb * LANES:(cb + 1) * LANES]

    rows = r_scr.shape[1] // dilation
    for r in range(dilation):
        for cb in range(n_col_blocks):
            o_ref[r, :, cb * LANES:(cb + 1) * LANES] = (
                r_scr[cb, pl.ds(r, rows, stride=dilation), :].astype(BF16))


def _qkv_projection(x2, mod_l, w_bf, tabs, bsz, seq, dilation):
    t, d = x2.shape
    tiles_per_seq = seq // PROJ_TM
    tab_spec = pl.BlockSpec((PROJ_TM, LANES), lambda i, j: (i, 0))
    return pl.pallas_call(
        functools.partial(_qkv_kernel, dilation=dilation),
        grid=(t // PROJ_TM, 3),
        in_specs=[
            pl.BlockSpec((PROJ_TM, d), lambda i, j: (i, 0)),
            pl.BlockSpec((None, N_MODULATIONS, d), lambda i, j: (i // tiles_per_seq, 0, 0)),
            pl.BlockSpec((d, GROUP_WIDTH), lambda i, j: (0, j)),
            tab_spec, tab_spec, tab_spec,
        ],
        out_specs=pl.BlockSpec((None, dilation, PROJ_TM // dilation, GROUP_WIDTH),
                               lambda i, j: (i // tiles_per_seq, 0, i % tiles_per_seq, j)),
        out_shape=jax.ShapeDtypeStruct((bsz, dilation, seq // dilation, 3 * GROUP_WIDTH), BF16),
        scratch_shapes=[pltpu.VMEM((PROJ_TM, d), BF16),
                        pltpu.VMEM((GROUP_WIDTH // LANES, PROJ_TM, LANES), F32)],
        compiler_params=_cparams(("arbitrary", "arbitrary")),
        name=f"qkv_projection_d{dilation}",
    )(x2, mod_l, w_bf, *tabs)


def _sg_in_kernel(x_ref, mod_ref, w_ref, b_ref, o_ref, h_scr):
    @pl.when(pl.program_id(1) == 0)
    def _():
        _modulate_into(h_scr, x_ref, mod_ref)

    z = jnp.dot(h_scr[...], w_ref[...], preferred_element_type=F32) + b_ref[...]
    o_ref[...] = (0.5 * z * (1.0 + lax.erf(z * (2.0 ** -0.5)))).astype(BF16)


def _sg_in_projection(x2, mod_l, w_bf, b_in, seq):
    t, d = x2.shape
    n = w_bf.shape[1]
    tiles_per_seq = seq // PROJ_TM
    return pl.pallas_call(
        _sg_in_kernel,
        grid=(t // PROJ_TM, n // PROJ_TN),
        in_specs=[
            pl.BlockSpec((PROJ_TM, d), lambda i, j: (i, 0)),
            pl.BlockSpec((None, N_MODULATIONS, d), lambda i, j: (i // tiles_per_seq, 0, 0)),
            pl.BlockSpec((d, PROJ_TN), lambda i, j: (0, j)),
            pl.BlockSpec((1, PROJ_TN), lambda i, j: (0, j)),
        ],
        out_specs=pl.BlockSpec((PROJ_TM, PROJ_TN), lambda i, j: (i, j)),
        out_shape=jax.ShapeDtypeStruct((t, n), BF16),
        scratch_shapes=[pltpu.VMEM((PROJ_TM, d), BF16)],
        compiler_params=_cparams(("arbitrary", "arbitrary")),
        name="sg_in_projection",
    )(x2, mod_l, w_bf, b_in.reshape(1, n))


def _attn_kernel(q_ref, kp_ref, kc_ref, vp_ref, vc_ref, o_ref, lse_ref, kcat, vcat, s_scr, p_scr, m_scr):
    n = pl.program_id(2)
    blk = q_ref.shape[0]
    n_pairs = GROUP_WIDTH // LANES
    lane = lax.broadcasted_iota(jnp.int32, (1, LANES), 1)
    keep = [jnp.where(lane < HEAD_DIM, 1.0, 0.0).astype(BF16), jnp.where(lane < HEAD_DIM, 0.0, 1.0).astype(BF16)]

    for pair in range(n_pairs):
        cs = slice(pair * LANES, (pair + 1) * LANES)
        for half in range(2):
            for j, (k_ref, v_ref) in enumerate(((kp_ref, vp_ref), (kc_ref, vc_ref))):
                rs = slice((2 * half + j) * blk, (2 * half + j + 1) * blk)
                kcat[pair, rs, :] = k_ref[:, cs] * keep[half]
                vcat[pair, rs, :] = v_ref[:, cs] * keep[half]

    dn = (((1,), (1,)), ((), ()))
    for pair in range(n_pairs):
        cs = slice(pair * LANES, (pair + 1) * LANES)
        q = q_ref[:, cs] * (HEAD_DIM ** -0.5)
        s_scr[pair] = lax.dot_general(q, kcat[pair], dn, preferred_element_type=F32)

    row = lax.broadcasted_iota(jnp.int32, (blk, 4 * blk), 0)
    col = lax.broadcasted_iota(jnp.int32, (blk, 4 * blk), 1)
    key = col & (blk - 1)
    is_prev = (col & blk) == 0
    ok = (is_prev & (key >= row) & (n > 0)) | (jnp.logical_not(is_prev) & (key <= row))
    first_head = col < 2 * blk
    low_full = lax.broadcasted_iota(jnp.int32, (blk, LANES), 1) < HEAD_DIM
    for pair in range(n_pairs):
        s = jnp.where(ok, s_scr[pair], NEG_BIG)
        m_a = jnp.max(s[:, :2 * blk], axis=-1, keepdims=True)
        m_b = jnp.max(s[:, 2 * blk:], axis=-1, keepdims=True)
        p_scr[pair] = jnp.exp(s - jnp.where(first_head, m_a, m_b)).astype(BF16)
        m_scr[pair] = jnp.where(low_full, m_a, m_b)

    r = lax.broadcasted_iota(jnp.int32, (4 * blk, LANES), 0)
    l = lax.broadcasted_iota(jnp.int32, (4 * blk, LANES), 1)
    head_sum = jnp.where((r < 2 * blk) == (l < HEAD_DIM), 1.0, 0.0).astype(BF16)
    for pair in range(n_pairs):
        cs = slice(pair * LANES, (pair + 1) * LANES)
        p = p_scr[pair]
        o = jnp.dot(p, vcat[pair], preferred_element_type=F32)
        den = jnp.dot(p, head_sum, preferred_element_type=F32)
        o_ref[:, cs] = (o / den).astype(BF16)
        lse_ref[:, cs] = m_scr[pair] + jnp.log(den)


def _dilated_attention(qkv, dilation):
    bsz, _, length, _ = qkv.shape
    nblk = length // ATT_BLOCK
    n_pairs = GROUP_WIDTH // LANES

    def spec(part, prev):
        def index(b, r, n):
            return (b, r, jnp.maximum(n - 1, 0) if prev else n, part)
        return pl.BlockSpec((None, None, ATT_BLOCK, GROUP_WIDTH), index)

    out_spec = pl.BlockSpec((None, None, ATT_BLOCK, GROUP_WIDTH), lambda b, r, n: (b, r, n, 0))
    return pl.pallas_call(
        _attn_kernel,
        grid=(bsz, dilation, nblk),
        in_specs=[spec(0, False), spec(1, True), spec(1, False), spec(2, True), spec(2, False)],
        out_specs=[out_spec, out_spec],
        out_shape=[jax.ShapeDtypeStruct((bsz, dilation, length, GROUP_WIDTH), BF16),
                   jax.ShapeDtypeStruct((bsz, dilation, length, GROUP_WIDTH), F32)],
        scratch_shapes=[
            pltpu.VMEM((n_pairs, 4 * ATT_BLOCK, LANES), BF16),
            pltpu.VMEM((n_pairs, 4 * ATT_BLOCK, LANES), BF16),
            pltpu.VMEM((n_pairs, ATT_BLOCK, 4 * ATT_BLOCK), F32),
            pltpu.VMEM((n_pairs, ATT_BLOCK, 4 * ATT_BLOCK), BF16),
            pltpu.VMEM((n_pairs, ATT_BLOCK, LANES), F32),
        ],
        compiler_params=_cparams(("parallel", "parallel", "parallel")),
        name=f"dilated_attention_d{dilation}",
    )(qkv, qkv, qkv, qkv, qkv)


def _merge_kernel(*refs):
    ng = len(DILATED_GROUPS)
    o_refs, l_refs, out_ref = refs[:ng], refs[ng:2 * ng], refs[2 * ng]
    scratch = refs[2 * ng + 1:]
    outs, lses = [], []
    for g, (_, dilation) in enumerate(DILATED_GROUPS):
        if dilation == 1:
            outs.append(o_refs[g][0].astype(F32))
            lses.append(l_refs[g][0])
            continue
        so, sl = scratch[2 * (g - 1)], scratch[2 * (g - 1) + 1]
        n_col_blocks = so.shape[0]
        rows = so.shape[1] // dilation
        for r in range(dilation):
            for cb in range(n_col_blocks):
                cs = slice(cb * LANES, (cb + 1) * LANES)
                so[cb, pl.ds(r, rows, stride=dilation), :] = o_refs[g][r, :, cs].astype(F32)
                sl[cb, pl.ds(r, rows, stride=dilation), :] = l_refs[g][r, :, cs]
        outs.append(jnp.concatenate([so[cb] for cb in range(n_col_blocks)], axis=1))
        lses.append(jnp.concatenate([sl[cb] for cb in range(n_col_blocks)], axis=1))
    mx = jnp.maximum(jnp.maximum(lses[0], lses[1]), lses[2])
    es = [jnp.exp(l - mx) for l in lses]
    inv = 1.0 / (es[0] + es[1] + es[2])
    for g in range(ng):
        out_ref[:, g * GROUP_WIDTH:(g + 1) * GROUP_WIDTH] = (outs[g] * (es[g] * inv)).astype(BF16)


def _merge_groups(outs, lses, seq):
    bsz = outs[0].shape[0]
    tiles_per_seq = seq // MERGE_TM
    specs = [pl.BlockSpec((None, dilation, MERGE_TM // dilation, GROUP_WIDTH),
                          lambda i: (i // tiles_per_seq, 0, i % tiles_per_seq, 0))
             for _, dilation in DILATED_GROUPS]
    assert DILATED_GROUPS[0][1] == 1
    scratch = []
    for _ in DILATED_GROUPS[1:]:
        scratch += [pltpu.VMEM((GROUP_WIDTH // LANES, MERGE_TM, LANES), F32)] * 2
    return pl.pallas_call(
        _merge_kernel,
        grid=(bsz * tiles_per_seq,),
        in_specs=specs + specs,
        out_specs=pl.BlockSpec((MERGE_TM, ATT_WIDTH), lambda i: (i, 0)),
        out_shape=jax.ShapeDtypeStruct((bsz * seq, ATT_WIDTH), BF16),
        scratch_shapes=scratch,
        compiler_params=_cparams(("parallel",)),
        name="merge_groups",
    )(*outs, *lses)


def _sgu_kernel(u_ref, v_ref, g_ref, b_ref, wsp_ref, bsp_ref, o_ref):
    v = v_ref[...].astype(F32)
    vn = _layer_norm(v, g_ref[...], b_ref[...]).astype(BF16)
    row = lax.broadcasted_iota(jnp.int32, (SG_CHUNK, SG_CHUNK), 0)
    col = lax.broadcasted_iota(jnp.int32, (SG_CHUNK, SG_CHUNK), 1)
    causal = col <= row
    for g in range(SG_GROUPS):
        w = jnp.where(causal, wsp_ref[g], 0.0).astype(BF16)
        bias = bsp_ref[:, g:g + 1]
        cs = slice(g * SG_GROUP_DIM, (g + 1) * SG_GROUP_DIM)
        for c in range(u_ref.shape[0] // SG_CHUNK):
            rs = slice(c * SG_CHUNK, (c + 1) * SG_CHUNK)
            mixed = jnp.dot(w, vn[rs, cs], preferred_element_type=F32) + bias
            o_ref[rs, cs] = (u_ref[rs, cs].astype(F32) * mixed).astype(BF16)


def _spatial_gating(z, ln_g, ln_b, w_sp, b_sp):
    t = z.shape[0]
    return pl.pallas_call(
        _sgu_kernel,
        grid=(t // SGU_TM,),
        in_specs=[
            pl.BlockSpec((SGU_TM, SG_WIDTH), lambda i: (i, 0)),
            pl.BlockSpec((SGU_TM, SG_WIDTH), lambda i: (i, 1)),
            pl.BlockSpec((1, SG_WIDTH), lambda i: (0, 0)),
            pl.BlockSpec((1, SG_WIDTH), lambda i: (0, 0)),
            pl.BlockSpec((SG_GROUPS, SG_CHUNK, SG_CHUNK), lambda i: (0, 0, 0)),
            pl.BlockSpec((SG_CHUNK, SG_GROUPS), lambda i: (0, 0)),
        ],
        out_specs=pl.BlockSpec((SGU_TM, SG_WIDTH), lambda i: (i, 0)),
        out_shape=jax.ShapeDtypeStruct((t, SG_WIDTH), BF16),
        compiler_params=_cparams(("parallel",)),
        name="spatial_gating",
    )(z, z, ln_g.reshape(1, -1), ln_b.reshape(1, -1), w_sp, b_sp.T)


def _proj_ln_kernel(a_ref, w_ref, x_ref, mod_ref, g_ref, b_ref, rw_hi_ref, rw_lo_ref, rb_ref,
                    x_out, h_out, logit_out):
    y = jnp.dot(a_ref[...], w_ref[...], preferred_element_type=F32)
    gate = mod_ref[2:3, :]
    xn = _layer_norm(DEEPNORM_ALPHA * x_ref[...] + (1.0 + gate) * y, g_ref[...], b_ref[...])
    x_out[...] = xn
    h = xn * (1.0 + mod_ref[4:5, :]) + mod_ref[3:4, :]
    _store_token_major(h_out, h)
    h_hi = h.astype(BF16)
    h_lo = (h - h_hi.astype(F32)).astype(BF16)
    rw_hi = rw_hi_ref[...]
    logits = jnp.dot(h_hi, rw_hi, preferred_element_type=F32)
    logits = logits + jnp.dot(h_hi, rw_lo_ref[...], preferred_element_type=F32)
    logits = logits + jnp.dot(h_lo, rw_hi, preferred_element_type=F32)
    logit_out[...] = logits + rb_ref[...]


def _proj_ln(a, w_bf, x2, mod_l, ln_g, ln_b, router_w, router_b, seq):
    t, d = x2.shape
    k = a.shape[1]
    tiles_per_seq = seq // LN_TM
    rw_hi = router_w.astype(BF16)
    rw_lo = (router_w - rw_hi.astype(F32)).astype(BF16)
    row = lambda i: (i, 0)
    fixed = lambda i: (0, 0)
    return pl.pallas_call(
        _proj_ln_kernel,
        grid=(t // LN_TM,),
        in_specs=[
            pl.BlockSpec((LN_TM, k), row),
            pl.BlockSpec((k, d), fixed, pipeline_mode=pl.Buffered(1)),
            pl.BlockSpec((LN_TM, d), row),
            pl.BlockSpec((None, N_MODULATIONS, d), lambda i: (i // tiles_per_seq, 0, 0)),
            pl.BlockSpec((1, d), fixed),
            pl.BlockSpec((1, d), fixed),
            pl.BlockSpec((d, N_EXPERTS), fixed),
            pl.BlockSpec((d, N_EXPERTS), fixed),
            pl.BlockSpec((1, N_EXPERTS), fixed),
        ],
        out_specs=[pl.BlockSpec((LN_TM, d), row), pl.BlockSpec((LN_TM * TOKEN_CHUNKS, LANES), row),
                   pl.BlockSpec((LN_TM, N_EXPERTS), row)],
        out_shape=[jax.ShapeDtypeStruct((t, d), F32), jax.ShapeDtypeStruct((t * TOKEN_CHUNKS, LANES), F32),
                   jax.ShapeDtypeStruct((t, N_EXPERTS), F32)],
        compiler_params=_cparams(("parallel",)),
        name="proj_ln",
    )(a, w_bf, x2, mod_l, ln_g.reshape(1, d), ln_b.reshape(1, d), rw_hi, rw_lo,
      router_b.reshape(1, N_EXPERTS))


def _topk_kernel(l_ref, idx_ref, p_ref):
    l = l_ref[...]
    tm = l.shape[0]
    lane = lax.broadcasted_iota(jnp.int32, l.shape, 1)
    out_lane = lax.broadcasted_iota(jnp.int32, (tm, TOP_K), 1)
    idx_out = jnp.zeros((tm, TOP_K), jnp.int32)
    val_out = jnp.zeros((tm, TOP_K), F32)
    for k in range(TOP_K):
        m = jnp.max(l, axis=-1, keepdims=True)
        idx = jnp.min(jnp.where(l == m, lane, N_EXPERTS), axis=-1, keepdims=True)
        idx_out = jnp.where(out_lane == k, idx, idx_out)
        val_out = jnp.where(out_lane == k, m, val_out)
        l = jnp.where(lane == idx, -jnp.inf, l)
    e = jnp.exp(val_out - jnp.max(val_out, axis=-1, keepdims=True))
    idx_ref[...] = idx_out
    p_ref[...] = e / jnp.sum(e, axis=-1, keepdims=True)


def _route_topk(logits):
    t = logits.shape[0]
    return pl.pallas_call(
        _topk_kernel,
        grid=(t // TOPK_TM,),
        in_specs=[pl.BlockSpec((TOPK_TM, N_EXPERTS), lambda i: (i, 0))],
        out_specs=[pl.BlockSpec((TOPK_TM, TOP_K), lambda i: (i, 0))] * 2,
        out_shape=[jax.ShapeDtypeStruct((t, TOP_K), jnp.int32), jax.ShapeDtypeStruct((t, TOP_K), F32)],
        compiler_params=_cparams(("parallel",)),
        name="route_topk",
    )(logits)


def _slab_copy(src_hbm, dst_vmem, sem, src_tok, dst_tok):
    src = pl.multiple_of(src_tok * TOKEN_CHUNKS, TOKEN_CHUNKS)
    return pltpu.make_async_copy(src_hbm.at[pl.ds(src, TOKEN_CHUNKS)],
                                 dst_vmem.at[pl.ds(dst_tok * SLAB_PITCH, TOKEN_CHUNKS)], sem)


def _gather_kernel(tok_ref, h_hbm, o_ref, buf, sem):
    i = pl.program_id(0)
    n_steps = pl.num_programs(0)

    def start_all(step, slot):
        def body(r2, carry):
            for prio in range(2):
                r = 2 * r2 + prio
                _slab_copy(h_hbm, buf.at[slot], sem.at[slot], tok_ref[step * GATHER_ROWS + r], r).start(
                    priority=prio)
            return carry
        lax.fori_loop(0, GATHER_ROWS // 2, body, 0, unroll=4)

    def wait_all(slot):
        def body(r, carry):
            _slab_copy(h_hbm, buf.at[slot], sem.at[slot], 0, r).wait()
            return carry
        lax.fori_loop(0, GATHER_ROWS, body, 0, unroll=8)

    @pl.when(i == 0)
    def _():
        start_all(0, 0)

    for slot in range(2):
        @pl.when((i % 2 == slot) & (i + 1 < n_steps))
        def _():
            start_all(i + 1, 1 - slot)

        @pl.when(i % 2 == slot)
        def _():
            wait_all(slot)
            o_ref[...] = _load_token_major(buf.at[slot], GATHER_ROWS, SLAB_PITCH).astype(BF16)


def _moe_gather(h_tm, row_tok):
    n_rows = row_tok.shape[0]
    return pl.pallas_call(
        _gather_kernel,
        grid_spec=pltpu.PrefetchScalarGridSpec(
            num_scalar_prefetch=1,
            grid=(n_rows // GATHER_ROWS,),
            in_specs=[pl.BlockSpec(memory_space=pl.ANY)],
            out_specs=pl.BlockSpec((GATHER_ROWS, D_MODEL), lambda i, tok: (i, 0)),
            scratch_shapes=[pltpu.VMEM((2, GATHER_ROWS * SLAB_PITCH, LANES), F32),
                            pltpu.SemaphoreType.DMA((2,))],
        ),
        out_shape=jax.ShapeDtypeStruct((n_rows, D_MODEL), BF16),
        compiler_params=_cparams(("arbitrary",)),
        name="moe_gather",
    )(row_tok, h_tm)


def _experts_kernel(we_ref, ws_ref, wn_ref, nw_ref, tail_ref, xs_hbm, win_hbm, wout_hbm, bin_ref, bout_ref, y_hbm,
                    wbuf, wbf, xbuf, hid, ystash, ybuf, wsem, xsem, ysem, *, layer):
    w = pl.program_id(0)
    p = pl.program_id(1)
    n_phase = FF_PHASES + OUT_PHASES
    n_work = nw_ref[0]
    active = w < n_work
    slot = (w * n_phase + p) % 2
    sub_rows = MOE_TM

    def weight_copy(src_hbm, expert, col_tile, sl, half):
        col = pl.multiple_of(col_tile * WEIGHT_TN, WEIGHT_TN)
        return pltpu.make_async_copy(src_hbm.at[layer, expert, :, pl.ds(col, WEIGHT_TN)],
                                     wbuf.at[sl, half], wsem.at[sl, half])

    def start_weights(item, phase, sl):
        expert = we_ref[item]

        @pl.when(phase < FF_PHASES)
        def _():
            weight_copy(win_hbm, expert, phase, sl, 0).start()
            weight_copy(win_hbm, expert, FF_PHASES + phase, sl, 1).start()

        @pl.when(phase >= FF_PHASES)
        def _():
            half_idx = phase - FF_PHASES
            weight_copy(wout_hbm, expert, 2 * half_idx, sl, 0).start()
            weight_copy(wout_hbm, expert, 2 * half_idx + 1, sl, 1).start()

    def x_copy(item, sub, xsl):
        row = pl.multiple_of(ws_ref[item] + sub * sub_rows, sub_rows)
        return pltpu.make_async_copy(xs_hbm.at[pl.ds(row, sub_rows)],
                                     xbuf.at[xsl, pl.ds(sub * sub_rows, sub_rows)], xsem.at[xsl])

    def for_each_x_copy(item, xsl, fn):
        for sub in range(EXPERT_MAX_ROWS // sub_rows):
            @pl.when(sub < wn_ref[item])
            def _():
                fn(x_copy(item, sub, xsl))

    def y_copy(item, sub, ysl):
        row = pl.multiple_of((ws_ref[item] + sub * sub_rows) * TOKEN_CHUNKS, sub_rows * TOKEN_CHUNKS)
        return pltpu.make_async_copy(ybuf.at[ysl], y_hbm.at[pl.ds(row, sub_rows * TOKEN_CHUNKS)], ysem.at[ysl])

    @pl.when((w == 0) & (p == 0))
    def _():
        start_weights(0, 0, 0)
        for_each_x_copy(0, 0, lambda cp: cp.start())
        ybuf[0] = jnp.zeros(ybuf.shape[1:], F32)

        def tail_copy(i):
            row = pl.multiple_of((tail_ref[0] + i) * (sub_rows * TOKEN_CHUNKS), sub_rows * TOKEN_CHUNKS)
            return pltpu.make_async_copy(ybuf.at[0], y_hbm.at[pl.ds(row, sub_rows * TOKEN_CHUNKS)], ysem.at[0])

        def start_tail(i, carry):
            tail_copy(i).start()
            return carry

        def wait_tail(i, carry):
            tail_copy(i).wait()
            return carry

        lax.fori_loop(0, tail_ref[1], start_tail, 0)
        lax.fori_loop(0, tail_ref[1], wait_tail, 0)

    last = p == n_phase - 1
    nxt_item = jnp.where(last, w + 1, w)
    nxt_phase = jnp.where(last, 0, p + 1)

    @pl.when(active & (nxt_item < n_work))
    def _():
        start_weights(nxt_item, nxt_phase, 1 - slot)

    @pl.when(active & (p == 0))
    def _():
        @pl.when(w + 1 < n_work)
        def _():
            for_each_x_copy(w + 1, (w + 1) % 2, lambda cp: cp.start())

        for_each_x_copy(w, w % 2, lambda cp: cp.wait())

    @pl.when(active)
    def _():
        for half in range(2):
            weight_copy(win_hbm, 0, 0, slot, half).wait()
            wbf[:, half * WEIGHT_TN:(half + 1) * WEIGHT_TN] = wbuf[slot, half].astype(BF16)
        n_sub = wn_ref[w]
        xsl = w % 2

        @pl.when(p < FF_PHASES)
        def _():
            bg = bin_ref[pl.ds(p, 1), :]
            bu = bin_ref[pl.ds(FF_PHASES + p, 1), :]

            def body(sub, carry):
                r0 = pl.multiple_of(sub * sub_rows, sub_rows)
                gu = jnp.dot(xbuf[xsl, pl.ds(r0, sub_rows), :], wbf[...], preferred_element_type=F32)
                gate = jnp.minimum(gu[:, :WEIGHT_TN] + bg, SWIGLU_LIMIT)
                up = jnp.clip(gu[:, WEIGHT_TN:] + bu, -SWIGLU_LIMIT, SWIGLU_LIMIT)
                hid[p, pl.ds(r0, sub_rows), :] = (
                    (up + 1.0) * gate * jax.nn.sigmoid(SWIGLU_ALPHA * gate)).astype(BF16)
                return carry

            lax.fori_loop(0, n_sub, body, 0)

        def out_half(sub, bo):
            r0 = pl.multiple_of(sub * sub_rows, sub_rows)
            acc = bo
            for f in range(FF_PHASES):
                acc = acc + jnp.dot(hid[f, pl.ds(r0, sub_rows), :], wbf[f * WEIGHT_TN:(f + 1) * WEIGHT_TN, :],
                                    preferred_element_type=F32)
            return r0, acc

        @pl.when(p == FF_PHASES)
        def _():
            bo = bout_ref[0:1, :]

            def body(sub, carry):
                r0, acc = out_half(sub, bo)
                ystash[pl.ds(r0, sub_rows), :] = acc
                return carry

            lax.fori_loop(0, n_sub, body, 0)

        @pl.when(p == FF_PHASES + 1)
        def _():
            bo = bout_ref[1:2, :]
            half_chunks = TOKEN_CHUNKS // 2

            def body(sub, carry):
                r0, acc = out_half(sub, bo)
                ysl = sub % 2

                @pl.when(sub >= 2)
                def _():
                    y_copy(w, sub, ysl).wait()

                stage = ybuf.at[ysl]
                for c in range(half_chunks):
                    cs = slice(c * LANES, (c + 1) * LANES)
                    stage[pl.ds(c, sub_rows, stride=TOKEN_CHUNKS), :] = ystash[pl.ds(r0, sub_rows), cs]
                    stage[pl.ds(half_chunks + c, sub_rows, stride=TOKEN_CHUNKS), :] = acc[:, cs]
                y_copy(w, sub, ysl).start()
                return carry

            lax.fori_loop(0, n_sub, body, 0)
            for back in (1, 2):
                @pl.when(n_sub >= back)
                def _():
                    y_copy(w, n_sub - back, (n_sub - back) % 2).wait()


def _expert_ffn(xs, work, w_in, b_in, w_out, b_out, layer):
    n_rows, d = xs.shape
    ne = w_in.shape[1]
    w_expert, w_start, w_nsub, n_work, tail = work
    n_items = w_expert.shape[0]
    bias_in = pl.BlockSpec((None, 2 * FF_PHASES, WEIGHT_TN), lambda w, p, we, *_: (we[w], 0, 0))
    bias_out = pl.BlockSpec((None, OUT_PHASES, 2 * WEIGHT_TN), lambda w, p, we, *_: (we[w], 0, 0))
    any_spec = pl.BlockSpec(memory_space=pl.ANY)
    return pl.pallas_call(
        functools.partial(_experts_kernel, layer=layer),
        grid_spec=pltpu.PrefetchScalarGridSpec(
            num_scalar_prefetch=5,
            grid=(n_items, FF_PHASES + OUT_PHASES),
            in_specs=[any_spec, any_spec, any_spec, bias_in, bias_out],
            out_specs=any_spec,
            scratch_shapes=[
                pltpu.VMEM((2, 2, d, WEIGHT_TN), F32),
                pltpu.VMEM((d, 2 * WEIGHT_TN), BF16),
                pltpu.VMEM((2, EXPERT_MAX_ROWS, d), BF16),
                pltpu.VMEM((FF_PHASES, EXPERT_MAX_ROWS, WEIGHT_TN), BF16),
                pltpu.VMEM((EXPERT_MAX_ROWS, 2 * WEIGHT_TN), F32),
                pltpu.VMEM((2, MOE_TM * TOKEN_CHUNKS, LANES), F32),
                pltpu.SemaphoreType.DMA((2, 2)),
                pltpu.SemaphoreType.DMA((2,)),
                pltpu.SemaphoreType.DMA((2,)),
            ],
        ),
        out_shape=jax.ShapeDtypeStruct((n_rows * TOKEN_CHUNKS, LANES), F32),
        compiler_params=_cparams(("arbitrary", "arbitrary")),
        name="expert_ffn",
    )(w_expert, w_start, w_nsub, n_work, tail, xs, w_in, w_out,
      b_in.reshape(ne, 2 * FF_PHASES, WEIGHT_TN), b_out.reshape(ne, OUT_PHASES, 2 * WEIGHT_TN))


def _combine_kernel(dest_ref, y_hbm, p_ref, x_ref, mod_ref, g_ref, b_ref, o_ref, buf, sem):
    i = pl.program_id(0)
    n_steps = pl.num_programs(0)

    def start_all(step, slot):
        def body(r, carry):
            for k in range(TOP_K):
                row = dest_ref[(step * COMBINE_TM + r) * TOP_K + k]
                _slab_copy(y_hbm, buf.at[slot, k], sem.at[slot], row, r).start(priority=k % 2)
            return carry
        lax.fori_loop(0, COMBINE_TM, body, 0, unroll=2)

    def wait_all(slot):
        def body(r, carry):
            for k in range(TOP_K):
                _slab_copy(y_hbm, buf.at[slot, k], sem.at[slot], 0, r).wait()
            return carry
        lax.fori_loop(0, COMBINE_TM, body, 0, unroll=2)

    @pl.when(i == 0)
    def _():
        start_all(0, 0)

    for slot in range(2):
        @pl.when((i % 2 == slot) & (i + 1 < n_steps))
        def _():
            start_all(i + 1, 1 - slot)

        @pl.when(i % 2 == slot)
        def _():
            wait_all(slot)
            p = p_ref[...]
            moe = p[:, 0:1] * _load_token_major(buf.at[slot, 0], COMBINE_TM, SLAB_PITCH)
            for k in range(1, TOP_K):
                moe = moe + p[:, k:k + 1] * _load_token_major(buf.at[slot, k], COMBINE_TM, SLAB_PITCH)
            gate = mod_ref[5:6, :]
            o_ref[...] = _layer_norm(DEEPNORM_ALPHA * x_ref[...] + (1.0 + gate) * moe, g_ref[...], b_ref[...])


def _combine_ln(y, dest, probs, x2, mod_l, ln_g, ln_b, seq):
    t, d = x2.shape
    tiles_per_seq = seq // COMBINE_TM
    return pl.pallas_call(
        _combine_kernel,
        grid_spec=pltpu.PrefetchScalarGridSpec(
            num_scalar_prefetch=1,
            grid=(t // COMBINE_TM,),
            in_specs=[
                pl.BlockSpec(memory_space=pl.ANY),
                pl.BlockSpec((COMBINE_TM, TOP_K), lambda i, dst: (i, 0)),
                pl.BlockSpec((COMBINE_TM, d), lambda i, dst: (i, 0)),
                pl.BlockSpec((None, N_MODULATIONS, d), lambda i, dst: (i // tiles_per_seq, 0, 0)),
                pl.BlockSpec((1, d), lambda i, dst: (0, 0)),
                pl.BlockSpec((1, d), lambda i, dst: (0, 0)),
            ],
            out_specs=pl.BlockSpec((COMBINE_TM, d), lambda i, dst: (i, 0)),
            scratch_shapes=[pltpu.VMEM((2, TOP_K, COMBINE_TM * SLAB_PITCH, LANES), F32),
                            pltpu.SemaphoreType.DMA((2,))],
        ),
        out_shape=jax.ShapeDtypeStruct((t, d), F32),
        compiler_params=_cparams(("arbitrary",)),
        name="combine_ln",
    )(dest, y, probs, x2, mod_l, ln_g.reshape(1, d), ln_b.reshape(1, d))


def _routing_tables(top_idx):
    t = top_idx.shape[0]
    flat_e = top_idx.reshape(-1)
    flat_t = jnp.repeat(jnp.arange(t, dtype=jnp.int32), TOP_K)
    onehot = jax.nn.one_hot(flat_e, N_EXPERTS, dtype=jnp.int32)
    counts = jnp.sum(onehot, axis=0)
    rank = jnp.take_along_axis(jnp.cumsum(onehot, axis=0) - onehot, flat_e[:, None], axis=1)[:, 0]
    padded = (counts + MOE_TM - 1) // MOE_TM * MOE_TM
    ends = jnp.cumsum(padded)
    starts = ends - padded
    dest = (starts[flat_e] + rank).astype(jnp.int32)
    n_rows = t * TOP_K + N_EXPERTS * MOE_TM
    row_tok = jnp.zeros((n_rows,), jnp.int32).at[dest].set(flat_t)
    chunks = (padded + EXPERT_MAX_ROWS - 1) // EXPERT_MAX_ROWS
    chunk_ends = jnp.cumsum(chunks)
    n_work = chunk_ends[-1:]
    item = jnp.arange(N_EXPERTS + n_rows // EXPERT_MAX_ROWS, dtype=jnp.int32)
    w_expert = jnp.minimum(jnp.sum(item[:, None] >= chunk_ends[None, :], axis=1), N_EXPERTS - 1)
    w_chunk = item - (chunk_ends[w_expert] - chunks[w_expert])
    live = item < n_work
    w_start = jnp.where(live, starts[w_expert] + w_chunk * EXPERT_MAX_ROWS, 0)
    w_rows = jnp.clip(padded[w_expert] - w_chunk * EXPERT_MAX_ROWS, 0, EXPERT_MAX_ROWS)
    w_nsub = jnp.where(live, w_rows // MOE_TM, 0)
    used_tiles = ends[-1] // MOE_TM
    tail = jnp.stack([used_tiles, n_rows // MOE_TM - used_tiles])
    work = tuple(a.astype(jnp.int32) for a in (w_expert, w_start, w_nsub, n_work, tail))
    return dest, row_tok, work


def _moe_block(h, logits, x2, mod_l, ln_g, ln_b, w_in, b_in, w_out, b_out, layer, seq):
    top_idx, probs = _route_topk(logits)
    dest, row_tok, work = _routing_tables(top_idx)
    xs = _moe_gather(h, row_tok)
    y = _expert_ffn(xs, work, w_in, b_in, w_out, b_out, layer)
    return _combine_ln(y, dest, probs, x2, mod_l, ln_g, ln_b, seq)


def kernel(x, c, positions, cond_w, cond_b, ln_g, ln_b, attn_w_qkv, attn_w_o, sg_w_in, sg_b_in, sg_ln_g, sg_ln_b, sg_w_spatial, sg_b_spatial, sg_w_out, router_w, router_b, expert_w_in, expert_b_in, expert_w_out, expert_b_out):
    bsz, seq, d = x.shape
    x2 = x.reshape(bsz * seq, d)
    mod = _modulation(c, cond_w, cond_b).reshape(DEPTH, bsz, N_MODULATIONS, d)
    tabs = _rope_tables(positions)

    w_qkv = attn_w_qkv[0].astype(BF16).reshape(d, 3, len(DILATED_GROUPS), GROUP_WIDTH)
    outs, lses = [], []
    for g, (_, dilation) in enumerate(DILATED_GROUPS):
        w_g = w_qkv[:, :, g, :].reshape(d, 3 * GROUP_WIDTH)
        qkv = _qkv_projection(x2, mod[0], w_g, tabs, bsz, seq, dilation)
        o, lse = _dilated_attention(qkv, dilation)
        outs.append(o)
        lses.append(lse)
    mixed = _merge_groups(outs, lses, seq)
    x2, h, logits = _proj_ln(mixed, attn_w_o[0].astype(BF16), x2, mod[0], ln_g[0, 0], ln_b[0, 0],
                             router_w[0], router_b[0], seq)
    x2 = _moe_block(h, logits, x2, mod[0], ln_g[0, 1], ln_b[0, 1], expert_w_in, expert_b_in[0],
                    expert_w_out, expert_b_out[0], 0, seq)

    z = _sg_in_projection(x2, mod[1], sg_w_in[0].astype(BF16), sg_b_in[0], seq)
    gated = _spatial_gating(z, sg_ln_g[0], sg_ln_b[0], sg_w_spatial[0], sg_b_spatial[0])
    x2, h, logits = _proj_ln(gated, sg_w_out[0].astype(BF16), x2, mod[1], ln_g[1, 0], ln_b[1, 0],
                             router_w[1], router_b[1], seq)
    x2 = _moe_block(h, logits, x2, mod[1], ln_g[1, 1], ln_b[1, 1], expert_w_in, expert_b_in[1],
                    expert_w_out, expert_b_out[1], 1, seq)
    return x2.reshape(bsz, seq, d)
```

```python
import functools

import jax
import jax.numpy as jnp
from jax import lax
from jax.experimental import pallas as pl
from jax.experimental.pallas import tpu as pltpu

F32 = jnp.float32
BF16 = jnp.bfloat16

D_MODEL = 2048
DEPTH = 2
HEAD_DIM = 64
HEADS_PER_GROUP = 16
DILATED_GROUPS = ((128, 1), (512, 4), (2048, 16))
GROUP_WIDTH = HEADS_PER_GROUP * HEAD_DIM
ATT_WIDTH = GROUP_WIDTH * len(DILATED_GROUPS)
ROPE_THETA = 500000.0
ROT_DIM = HEAD_DIM // 4
ATT_BLOCK = 128

SG_CHUNK = 128
SG_WIDTH = 2 * D_MODEL
SG_GROUPS = 16
SG_GROUP_DIM = SG_WIDTH // SG_GROUPS

N_EXPERTS = 32
TOP_K = 4
EXPERT_FF = D_MODEL
SWIGLU_LIMIT = 7.0
SWIGLU_ALPHA = 1.702

N_MODULATIONS = 6
DEEPNORM_ALPHA = (2 * DEPTH) ** 0.25
LN_EPS = 1e-5
NEG_BIG = -1e30

LANES = 128
TOKEN_CHUNKS = D_MODEL // LANES
SLAB_PITCH = TOKEN_CHUNKS + 1
VMEM_LIMIT_BYTES = 56 * 1024 * 1024

MOD_TN = 512
ROPE_TM = 1024
PROJ_TM = 1024
PROJ_TN = 1024
MERGE_TM = 512
LN_TM = 256
TOPK_TM = 1024
SGU_TM = 256
MOE_TM = 256
WEIGHT_TN = 512
FF_PHASES = EXPERT_FF // WEIGHT_TN
OUT_PHASES = D_MODEL // (2 * WEIGHT_TN)
EXPERT_MAX_ROWS = 5 * MOE_TM
GATHER_ROWS = 512
COMBINE_TM = 256


def _cparams(sem):
    return pltpu.CompilerParams(dimension_semantics=sem, vmem_limit_bytes=VMEM_LIMIT_BYTES)


def _store_token_major(ref, val):
    rows = val.shape[0]
    for c in range(TOKEN_CHUNKS):
        ref[pl.ds(c, rows, stride=TOKEN_CHUNKS), :] = val[:, c * LANES:(c + 1) * LANES]


def _load_token_major(ref, rows, pitch=TOKEN_CHUNKS):
    return jnp.concatenate(
        [ref[pl.ds(c, rows, stride=pitch), :] for c in range(TOKEN_CHUNKS)], axis=1)


def _layer_norm(z, g, b):
    mu = jnp.mean(z, axis=-1, keepdims=True)
    zc = z - mu
    var = jnp.mean(zc * zc, axis=-1, keepdims=True)
    return zc * lax.rsqrt(var + LN_EPS) * g + b


def _mod_kernel(ct_ref, w_ref, b_ref, o_ref):
    ct = ct_ref[...]
    ca = ct * jax.nn.sigmoid(ct)
    w = w_ref[...]
    for b in range(ct.shape[1]):
        o_ref[b:b + 1, :] = jnp.sum(ca[:, b:b + 1] * w, axis=0, keepdims=True) + b_ref[...]


def _modulation(c, cond_w, cond_b):
    nl, d, n = cond_w.shape
    bsz = c.shape[0]
    return pl.pallas_call(
        _mod_kernel,
        grid=(nl, n // MOD_TN),
        in_specs=[
            pl.BlockSpec((d, bsz), lambda l, j: (0, 0)),
            pl.BlockSpec((None, d, MOD_TN), lambda l, j: (l, 0, j)),
            pl.BlockSpec((None, 1, MOD_TN), lambda l, j: (l, 0, j)),
        ],
        out_specs=pl.BlockSpec((None, bsz, MOD_TN), lambda l, j: (l, 0, j)),
        out_shape=jax.ShapeDtypeStruct((nl, bsz, n), F32),
        compiler_params=_cparams(("parallel", "parallel")),
        name="modulation",
    )(c.T, cond_w, cond_b.reshape(nl, 1, n))


def _rope_kernel(pos_ref, inv_ref, c_ref, s1_ref, s2_ref):
    ang = pos_ref[...] * inv_ref[...]
    lane = lax.broadcasted_iota(jnp.int32, ang.shape, 1) & (HEAD_DIM - 1)
    cs = jnp.cos(ang)
    sn = jnp.sin(ang)
    half = ROT_DIM // 2
    c_ref[...] = jnp.where(lane < ROT_DIM, cs, 1.0)
    s1_ref[...] = jnp.where(lane < half, -sn, 0.0)
    s2_ref[...] = jnp.where((lane >= half) & (lane < ROT_DIM), sn, 0.0)


def _rope_tables(positions):
    t = positions.size
    pos = positions.astype(F32).reshape(t, 1)
    inv = jnp.power(jnp.float32(ROPE_THETA), -jnp.arange(0, ROT_DIM, 2, dtype=F32) / ROT_DIM)
    lane = jnp.arange(LANES) % HEAD_DIM
    inv_row = inv[lane % (ROT_DIM // 2)].reshape(1, LANES)
    spec = pl.BlockSpec((ROPE_TM, LANES), lambda i: (i, 0))
    shp = jax.ShapeDtypeStruct((t, LANES), F32)
    return pl.pallas_call(
        _rope_kernel,
        grid=(t // ROPE_TM,),
        in_specs=[pl.BlockSpec((ROPE_TM, 1), lambda i: (i, 0)), pl.BlockSpec((1, LANES), lambda i: (0, 0))],
        out_specs=[spec, spec, spec],
        out_shape=[shp, shp, shp],
        compiler_params=_cparams(("parallel",)),
        name="rope_tables",
    )(pos, inv_row)


def _modulate_into(h_scr, x_ref, mod_ref):
    shift = mod_ref[0:1, :]
    scale = mod_ref[1:2, :]
    h_scr[...] = (x_ref[...] * (1.0 + scale) + shift).astype(BF16)


def _qkv_kernel(x_ref, mod_ref, w_ref, c_ref, s1_ref, s2_ref, o_ref, h_scr, r_scr, *, dilation):
    j = pl.program_id(1)

    @pl.when(j == 0)
    def _():
        _modulate_into(h_scr, x_ref, mod_ref)

    acc = jnp.dot(h_scr[...], w_ref[...], preferred_element_type=F32)
    n_col_blocks = acc.shape[1] // LANES

    @pl.when(j < 2)
    def _():
        c = c_ref[...]
        s1 = s1_ref[...]
        s2 = s2_ref[...]
        for cb in range(n_col_blocks):
            a = acc[:, cb * LANES:(cb + 1) * LANES]
            up = pltpu.roll(a, LANES - ROT_DIM // 2, 1)
            dn = pltpu.roll(a, ROT_DIM // 2, 1)
            r_scr[cb] = a * c + up * s1 + dn * s2

    @pl.when(j == 2)
    def _():
        for cb in range(n_col_blocks):
            r_scr[cb] = acc[:, cb * LANES:(cb + 1) * LANES]

    rows = r_scr.shape[1] // dilation
    for r in range(dilation):
        for cb in range(n_col_blocks):
            o_ref[r, :, cb * LANES:(cb + 1) * LANES] = (
                r_scr[cb, pl.ds(r, rows, stride=dilation), :].astype(BF16))


def _qkv_projection(x2, mod_l, w_bf, tabs, bsz, seq, dilation):
    t, d = x2.shape
    tiles_per_seq = seq // PROJ_TM
    tab_spec = pl.BlockSpec((PROJ_TM, LANES), lambda i, j: (i, 0))
    return pl.pallas_call(
        functools.partial(_qkv_kernel, dilation=dilation),
        grid=(t // PROJ_TM, 3),
        in_specs=[
            pl.BlockSpec((PROJ_TM, d), lambda i, j: (i, 0)),
            pl.BlockSpec((None, N_MODULATIONS, d), lambda i, j: (i // tiles_per_seq, 0, 0)),
            pl.BlockSpec((d, GROUP_WIDTH), lambda i, j: (0, j)),
            tab_spec, tab_spec, tab_spec,
        ],
        out_specs=pl.BlockSpec((None, dilation, PROJ_TM // dilation, GROUP_WIDTH),
                               lambda i, j: (i // tiles_per_seq, 0, i % tiles_per_seq, j)),
        out_shape=jax.ShapeDtypeStruct((bsz, dilation, seq // dilation, 3 * GROUP_WIDTH), BF16),
        scratch_shapes=[pltpu.VMEM((PROJ_TM, d), BF16),
                        pltpu.VMEM((GROUP_WIDTH // LANES, PROJ_TM, LANES), F32)],
        compiler_params=_cparams(("arbitrary", "arbitrary")),
        name=f"qkv_projection_d{dilation}",
    )(x2, mod_l, w_bf, *tabs)


def _sg_in_kernel(x_ref, mod_ref, w_ref, b_ref, o_ref, h_scr):
    @pl.when(pl.program_id(1) == 0)
    def _():
        _modulate_into(h_scr, x_ref, mod_ref)

    z = jnp.dot(h_scr[...], w_ref[...], preferred_element_type=F32) + b_ref[...]
    o_ref[...] = (0.5 * z * (1.0 + lax.erf(z * (2.0 ** -0.5)))).astype(BF16)


def _sg_in_projection(x2, mod_l, w_bf, b_in, seq):
    t, d = x2.shape
    n = w_bf.shape[1]
    tiles_per_seq = seq // PROJ_TM
    return pl.pallas_call(
        _sg_in_kernel,
        grid=(t // PROJ_TM, n // PROJ_TN),
        in_specs=[
            pl.BlockSpec((PROJ_TM, d), lambda i, j: (i, 0)),
            pl.BlockSpec((None, N_MODULATIONS, d), lambda i, j: (i // tiles_per_seq, 0, 0)),
            pl.BlockSpec((d, PROJ_TN), lambda i, j: (0, j)),
            pl.BlockSpec((1, PROJ_TN), lambda i, j: (0, j)),
        ],
        out_specs=pl.BlockSpec((PROJ_TM, PROJ_TN), lambda i, j: (i, j)),
        out_shape=jax.ShapeDtypeStruct((t, n), BF16),
        scratch_shapes=[pltpu.VMEM((PROJ_TM, d), BF16)],
        compiler_params=_cparams(("arbitrary", "arbitrary")),
        name="sg_in_projection",
    )(x2, mod_l, w_bf, b_in.reshape(1, n))


def _attn_kernel(q_ref, kp_ref, kc_ref, vp_ref, vc_ref, o_ref, lse_ref, kcat, vcat, s_scr, p_scr, m_scr):
    n = pl.program_id(2)
    blk = q_ref.shape[0]
    n_pairs = GROUP_WIDTH // LANES
    lane = lax.broadcasted_iota(jnp.int32, (1, LANES), 1)
    keep = [jnp.where(lane < HEAD_DIM, 1.0, 0.0).astype(BF16), jnp.where(lane < HEAD_DIM, 0.0, 1.0).astype(BF16)]

    for pair in range(n_pairs):
        cs = slice(pair * LANES, (pair + 1) * LANES)
        for half in range(2):
            for j, (k_ref, v_ref) in enumerate(((kp_ref, vp_ref), (kc_ref, vc_ref))):
                rs = slice((2 * half + j) * blk, (2 * half + j + 1) * blk)
                kcat[pair, rs, :] = k_ref[:, cs] * keep[half]
                vcat[pair, rs, :] = v_ref[:, cs] * keep[half]

    dn = (((1,), (1,)), ((), ()))
    for pair in range(n_pairs):
        cs = slice(pair * LANES, (pair + 1) * LANES)
        q = q_ref[:, cs] * (HEAD_DIM ** -0.5)
        s_scr[pair] = lax.dot_general(q, kcat[pair], dn, preferred_element_type=F32)

    row = lax.broadcasted_iota(jnp.int32, (blk, 4 * blk), 0)
    col = lax.broadcasted_iota(jnp.int32, (blk, 4 * blk), 1)
    key = col & (blk - 1)
    is_prev = (col & blk) == 0
    ok = (is_prev & (key >= row) & (n > 0)) | (jnp.logical_not(is_prev) & (key <= row))
    first_head = col < 2 * blk
    low_full = lax.broadcasted_iota(jnp.int32, (blk, LANES), 1) < HEAD_DIM
    for pair in range(n_pairs):
        s = jnp.where(ok, s_scr[pair], NEG_BIG)
        m_a = jnp.max(s[:, :2 * blk], axis=-1, keepdims=True)
        m_b = jnp.max(s[:, 2 * blk:], axis=-1, keepdims=True)
        p_scr[pair] = jnp.exp(s - jnp.where(first_head, m_a, m_b)).astype(BF16)
        m_scr[pair] = jnp.where(low_full, m_a, m_b)

    r = lax.broadcasted_iota(jnp.int32, (4 * blk, LANES), 0)
    l = lax.broadcasted_iota(jnp.int32, (4 * blk, LANES), 1)
    head_sum = jnp.where((r < 2 * blk) == (l < HEAD_DIM), 1.0, 0.0).astype(BF16)
    for pair in range(n_pairs):
        cs = slice(pair * LANES, (pair + 1) * LANES)
        p = p_scr[pair]
        o = jnp.dot(p, vcat[pair], preferred_element_type=F32)
        den = jnp.dot(p, head_sum, preferred_element_type=F32)
        o_ref[:, cs] = (o / den).astype(BF16)
        lse_ref[:, cs] = m_scr[pair] + jnp.log(den)


def _dilated_attention(qkv, dilation):
    bsz, _, length, _ = qkv.shape
    nblk = length // ATT_BLOCK
    n_pairs = GROUP_WIDTH // LANES

    def spec(part, prev):
        def index(b, r, n):
            return (b, r, jnp.maximum(n - 1, 0) if prev else n, part)
        return pl.BlockSpec((None, None, ATT_BLOCK, GROUP_WIDTH), index)

    out_spec = pl.BlockSpec((None, None, ATT_BLOCK, GROUP_WIDTH), lambda b, r, n: (b, r, n, 0))
    return pl.pallas_call(
        _attn_kernel,
        grid=(bsz, dilation, nblk),
        in_specs=[spec(0, False), spec(1, True), spec(1, False), spec(2, True), spec(2, False)],
        out_specs=[out_spec, out_spec],
        out_shape=[jax.ShapeDtypeStruct((bsz, dilation, length, GROUP_WIDTH), BF16),
                   jax.ShapeDtypeStruct((bsz, dilation, length, GROUP_WIDTH), F32)],
        scratch_shapes=[
            pltpu.VMEM((n_pairs, 4 * ATT_BLOCK, LANES), BF16),
            pltpu.VMEM((n_pairs, 4 * ATT_BLOCK, LANES), BF16),
            pltpu.VMEM((n_pairs, ATT_BLOCK, 4 * ATT_BLOCK), F32),
            pltpu.VMEM((n_pairs, ATT_BLOCK, 4 * ATT_BLOCK), BF16),
            pltpu.VMEM((n_pairs, ATT_BLOCK, LANES), F32),
        ],
        compiler_params=_cparams(("parallel", "parallel", "parallel")),
        name=f"dilated_attention_d{dilation}",
    )(qkv, qkv, qkv, qkv, qkv)


def _merge_kernel(*refs):
    ng = len(DILATED_GROUPS)
    o_refs, l_refs, out_ref = refs[:ng], refs[ng:2 * ng], refs[2 * ng]
    scratch = refs[2 * ng + 1:]
    outs, lses = [], []
    for g, (_, dilation) in enumerate(DILATED_GROUPS):
        if dilation == 1:
            outs.append(o_refs[g][0].astype(F32))
            lses.append(l_refs[g][0])
            continue
        so, sl = scratch[2 * (g - 1)], scratch[2 * (g - 1) + 1]
        n_col_blocks = so.shape[0]
        rows = so.shape[1] // dilation
        for r in range(dilation):
            for cb in range(n_col_blocks):
                cs = slice(cb * LANES, (cb + 1) * LANES)
                so[cb, pl.ds(r, rows, stride=dilation), :] = o_refs[g][r, :, cs].astype(F32)
                sl[cb, pl.ds(r, rows, stride=dilation), :] = l_refs[g][r, :, cs]
        outs.append(jnp.concatenate([so[cb] for cb in range(n_col_blocks)], axis=1))
        lses.append(jnp.concatenate([sl[cb] for cb in range(n_col_blocks)], axis=1))
    mx = jnp.maximum(jnp.maximum(lses[0], lses[1]), lses[2])
    es = [jnp.exp(l - mx) for l in lses]
    inv = 1.0 / (es[0] + es[1] + es[2])
    for g in range(ng):
        out_ref[:, g * GROUP_WIDTH:(g + 1) * GROUP_WIDTH] = (outs[g] * (es[g] * inv)).astype(BF16)


def _merge_groups(outs, lses, seq):
    bsz = outs[0].shape[0]
    tiles_per_seq = seq // MERGE_TM
    specs = [pl.BlockSpec((None, dilation, MERGE_TM // dilation, GROUP_WIDTH),
                          lambda i: (i // tiles_per_seq, 0, i % tiles_per_seq, 0))
             for _, dilation in DILATED_GROUPS]
    assert DILATED_GROUPS[0][1] == 1
    scratch = []
    for _ in DILATED_GROUPS[1:]:
        scratch += [pltpu.VMEM((GROUP_WIDTH // LANES, MERGE_TM, LANES), F32)] * 2
    return pl.pallas_call(
        _merge_kernel,
        grid=(bsz * tiles_per_seq,),
        in_specs=specs + specs,
        out_specs=pl.BlockSpec((MERGE_TM, ATT_WIDTH), lambda i: (i, 0)),
        out_shape=jax.ShapeDtypeStruct((bsz * seq, ATT_WIDTH), BF16),
        scratch_shapes=scratch,
        compiler_params=_cparams(("parallel",)),
        name="merge_groups",
    )(*outs, *lses)


def _sgu_kernel(u_ref, v_ref, g_ref, b_ref, wsp_ref, bsp_ref, o_ref):
    v = v_ref[...].astype(F32)
    vn = _layer_norm(v, g_ref[...], b_ref[...]).astype(BF16)
    row = lax.broadcasted_iota(jnp.int32, (SG_CHUNK, SG_CHUNK), 0)
    col = lax.broadcasted_iota(jnp.int32, (SG_CHUNK, SG_CHUNK), 1)
    causal = col <= row
    for g in range(SG_GROUPS):
        w = jnp.where(causal, wsp_ref[g], 0.0).astype(BF16)
        bias = bsp_ref[:, g:g + 1]
        cs = slice(g * SG_GROUP_DIM, (g + 1) * SG_GROUP_DIM)
        for c in range(u_ref.shape[0] // SG_CHUNK):
            rs = slice(c * SG_CHUNK, (c + 1) * SG_CHUNK)
            mixed = jnp.dot(w, vn[rs, cs], preferred_element_type=F32) + bias
            o_ref[rs, cs] = (u_ref[rs, cs].astype(F32) * mixed).astype(BF16)


def _spatial_gating(z, ln_g, ln_b, w_sp, b_sp):
    t = z.shape[0]
    return pl.pallas_call(
        _sgu_kernel,
        grid=(t // SGU_TM,),
        in_specs=[
            pl.BlockSpec((SGU_TM, SG_WIDTH), lambda i: (i, 0)),
            pl.BlockSpec((SGU_TM, SG_WIDTH), lambda i: (i, 1)),
            pl.BlockSpec((1, SG_WIDTH), lambda i: (0, 0)),
            pl.BlockSpec((1, SG_WIDTH), lambda i: (0, 0)),
            pl.BlockSpec((SG_GROUPS, SG_CHUNK, SG_CHUNK), lambda i: (0, 0, 0)),
            pl.BlockSpec((SG_CHUNK, SG_GROUPS), lambda i: (0, 0)),
        ],
        out_specs=pl.BlockSpec((SGU_TM, SG_WIDTH), lambda i: (i, 0)),
        out_shape=jax.ShapeDtypeStruct((t, SG_WIDTH), BF16),
        compiler_params=_cparams(("parallel",)),
        name="spatial_gating",
    )(z, z, ln_g.reshape(1, -1), ln_b.reshape(1, -1), w_sp, b_sp.T)


def _proj_ln_kernel(a_ref, w_ref, x_ref, mod_ref, g_ref, b_ref, rw_hi_ref, rw_lo_ref, rb_ref,
                    x_out, h_out, logit_out):
    y = jnp.dot(a_ref[...], w_ref[...], preferred_element_type=F32)
    gate = mod_ref[2:3, :]
    xn = _layer_norm(DEEPNORM_ALPHA * x_ref[...] + (1.0 + gate) * y, g_ref[...], b_ref[...])
    x_out[...] = xn
    h = xn * (1.0 + mod_ref[4:5, :]) + mod_ref[3:4, :]
    _store_token_major(h_out, h)
    h_hi = h.astype(BF16)
    h_lo = (h - h_hi.astype(F32)).astype(BF16)
    rw_hi = rw_hi_ref[...]
    logits = jnp.dot(h_hi, rw_hi, preferred_element_type=F32)
    logits = logits + jnp.dot(h_hi, rw_lo_ref[...], preferred_element_type=F32)
    logits = logits + jnp.dot(h_lo, rw_hi, preferred_element_type=F32)
    logit_out[...] = logits + rb_ref[...]


def _proj_ln(a, w_bf, x2, mod_l, ln_g, ln_b, router_w, router_b, seq):
    t, d = x2.shape
    k = a.shape[1]
    tiles_per_seq = seq // LN_TM
    rw_hi = router_w.astype(BF16)
    rw_lo = (router_w - rw_hi.astype(F32)).astype(BF16)
    row = lambda i: (i, 0)
    fixed = lambda i: (0, 0)
    return pl.pallas_call(
        _proj_ln_kernel,
        grid=(t // LN_TM,),
        in_specs=[
            pl.BlockSpec((LN_TM, k), row),
            pl.BlockSpec((k, d), fixed, pipeline_mode=pl.Buffered(1)),
            pl.BlockSpec((LN_TM, d), row),
            pl.BlockSpec((None, N_MODULATIONS, d), lambda i: (i // tiles_per_seq, 0, 0)),
            pl.BlockSpec((1, d), fixed),
            pl.BlockSpec((1, d), fixed),
            pl.BlockSpec((d, N_EXPERTS), fixed),
            pl.BlockSpec((d, N_EXPERTS), fixed),
            pl.BlockSpec((1, N_EXPERTS), fixed),
        ],
        out_specs=[pl.BlockSpec((LN_TM, d), row), pl.BlockSpec((LN_TM * TOKEN_CHUNKS, LANES), row),
                   pl.BlockSpec((LN_TM, N_EXPERTS), row)],
        out_shape=[jax.ShapeDtypeStruct((t, d), F32), jax.ShapeDtypeStruct((t * TOKEN_CHUNKS, LANES), F32),
                   jax.ShapeDtypeStruct((t, N_EXPERTS), F32)],
        compiler_params=_cparams(("parallel",)),
        name="proj_ln",
    )(a, w_bf, x2, mod_l, ln_g.reshape(1, d), ln_b.reshape(1, d), rw_hi, rw_lo,
      router_b.reshape(1, N_EXPERTS))


def _topk_kernel(l_ref, idx_ref, p_ref):
    l = l_ref[...]
    tm = l.shape[0]
    lane = lax.broadcasted_iota(jnp.int32, l.shape, 1)
    out_lane = lax.broadcasted_iota(jnp.int32, (tm, TOP_K), 1)
    idx_out = jnp.zeros((tm, TOP_K), jnp.int32)
    val_out = jnp.zeros((tm, TOP_K), F32)
    for k in range(TOP_K):
        m = jnp.max(l, axis=-1, keepdims=True)
        idx = jnp.min(jnp.where(l == m, lane, N_EXPERTS), axis=-1, keepdims=True)
        idx_out = jnp.where(out_lane == k, idx, idx_out)
        val_out = jnp.where(out_lane == k, m, val_out)
        l = jnp.where(lane == idx, -jnp.inf, l)
    e = jnp.exp(val_out - jnp.max(val_out, axis=-1, keepdims=True))
    idx_ref[...] = idx_out
    p_ref[...] = e / jnp.sum(e, axis=-1, keepdims=True)


def _route_topk(logits):
    t = logits.shape[0]
    return pl.pallas_call(
        _topk_kernel,
        grid=(t // TOPK_TM,),
        in_specs=[pl.BlockSpec((TOPK_TM, N_EXPERTS), lambda i: (i, 0))],
        out_specs=[pl.BlockSpec((TOPK_TM, TOP_K), lambda i: (i, 0))] * 2,
        out_shape=[jax.ShapeDtypeStruct((t, TOP_K), jnp.int32), jax.ShapeDtypeStruct((t, TOP_K), F32)],
        compiler_params=_cparams(("parallel",)),
        name="route_topk",
    )(logits)


def _slab_copy(src_hbm, dst_vmem, sem, src_tok, dst_tok):
    src = pl.multiple_of(src_tok * TOKEN_CHUNKS, TOKEN_CHUNKS)
    return pltpu.make_async_copy(src_hbm.at[pl.ds(src, TOKEN_CHUNKS)],
                                 dst_vmem.at[pl.ds(dst_tok * SLAB_PITCH, TOKEN_CHUNKS)], sem)


def _gather_kernel(tok_ref, h_hbm, o_ref, buf, sem):
    i = pl.program_id(0)
    n_steps = pl.num_programs(0)

    def start_all(step, slot):
        def body(r2, carry):
            for prio in range(2):
                r = 2 * r2 + prio
                _slab_copy(h_hbm, buf.at[slot], sem.at[slot], tok_ref[step * GATHER_ROWS + r], r).start(
                    priority=prio)
            return carry
        lax.fori_loop(0, GATHER_ROWS // 2, body, 0, unroll=4)

    def wait_all(slot):
        def body(r, carry):
            _slab_copy(h_hbm, buf.at[slot], sem.at[slot], 0, r).wait()
            return carry
        lax.fori_loop(0, GATHER_ROWS, body, 0, unroll=8)

    @pl.when(i == 0)
    def _():
        start_all(0, 0)

    for slot in range(2):
        @pl.when((i % 2 == slot) & (i + 1 < n_steps))
        def _():
            start_all(i + 1, 1 - slot)

        @pl.when(i % 2 == slot)
        def _():
            wait_all(slot)
            o_ref[...] = _load_token_major(buf.at[slot], GATHER_ROWS, SLAB_PITCH).astype(BF16)


def _moe_gather(h_tm, row_tok):
    n_rows = row_tok.shape[0]
    return pl.pallas_call(
        _gather_kernel,
        grid_spec=pltpu.PrefetchScalarGridSpec(
            num_scalar_prefetch=1,
            grid=(n_rows // GATHER_ROWS,),
            in_specs=[pl.BlockSpec(memory_space=pl.ANY)],
            out_specs=pl.BlockSpec((GATHER_ROWS, D_MODEL), lambda i, tok: (i, 0)),
            scratch_shapes=[pltpu.VMEM((2, GATHER_ROWS * SLAB_PITCH, LANES), F32),
                            pltpu.SemaphoreType.DMA((2,))],
        ),
        out_shape=jax.ShapeDtypeStruct((n_rows, D_MODEL), BF16),
        compiler_params=_cparams(("arbitrary",)),
        name="moe_gather",
    )(row_tok, h_tm)


def _experts_kernel(we_ref, ws_ref, wn_ref, nw_ref, tail_ref, xs_hbm, win_hbm, wout_hbm, bin_ref, bout_ref, y_hbm,
                    wbuf, wbf, xbuf, hid, ystash, ybuf, wsem, xsem, ysem, *, layer):
    w = pl.program_id(0)
    n_phase = FF_PHASES + OUT_PHASES
    assert n_phase % 2 == 0
    n_work = nw_ref[0]
    active = w < n_work
    sub_rows = MOE_TM

    def weight_copy(src_hbm, expert, col_tile, sl, half):
        return pltpu.make_async_copy(src_hbm.at[layer, expert, :, pl.ds(col_tile * WEIGHT_TN, WEIGHT_TN)],
                                     wbuf.at[sl, half], wsem.at[sl, half])

    def start_weights(item, phase):
        expert = we_ref[item]
        sl = phase % 2
        if phase < FF_PHASES:
            weight_copy(win_hbm, expert, phase, sl, 0).start()
            weight_copy(win_hbm, expert, FF_PHASES + phase, sl, 1).start()
        else:
            half_idx = phase - FF_PHASES
            weight_copy(wout_hbm, expert, 2 * half_idx, sl, 0).start()
            weight_copy(wout_hbm, expert, 2 * half_idx + 1, sl, 1).start()

    def x_copy(item, sub, xsl):
        row = pl.multiple_of(ws_ref[item] + sub * sub_rows, sub_rows)
        return pltpu.make_async_copy(xs_hbm.at[pl.ds(row, sub_rows)],
                                     xbuf.at[xsl, pl.ds(sub * sub_rows, sub_rows)], xsem.at[xsl])

    def for_each_x_copy(item, xsl, fn):
        for sub in range(EXPERT_MAX_ROWS // sub_rows):
            @pl.when(sub < wn_ref[item])
            def _():
                fn(x_copy(item, sub, xsl))

    def y_copy(item, sub, ysl):
        row = pl.multiple_of((ws_ref[item] + sub * sub_rows) * TOKEN_CHUNKS, sub_rows * TOKEN_CHUNKS)
        return pltpu.make_async_copy(ybuf.at[ysl], y_hbm.at[pl.ds(row, sub_rows * TOKEN_CHUNKS)], ysem.at[ysl])

    @pl.when(w == 0)
    def _():
        start_weights(0, 0)
        for_each_x_copy(0, 0, lambda cp: cp.start())
        ybuf[0] = jnp.zeros(ybuf.shape[1:], F32)

        def tail_copy(i):
            row = pl.multiple_of((tail_ref[0] + i) * (sub_rows * TOKEN_CHUNKS), sub_rows * TOKEN_CHUNKS)
            return pltpu.make_async_copy(ybuf.at[0], y_hbm.at[pl.ds(row, sub_rows * TOKEN_CHUNKS)], ysem.at[0])

        def start_tail(i, carry):
            tail_copy(i).start()
            return carry

        def wait_tail(i, carry):
            tail_copy(i).wait()
            return carry

        lax.fori_loop(0, tail_ref[1], start_tail, 0)
        lax.fori_loop(0, tail_ref[1], wait_tail, 0)

    def ff_phase(f, n_sub, xsl):
        bg = bin_ref[f:f + 1, :]
        bu = bin_ref[FF_PHASES + f:FF_PHASES + f + 1, :]

        def body(sub, carry):
            r0 = pl.multiple_of(sub * sub_rows, sub_rows)
            gu = jnp.dot(xbuf[xsl, pl.ds(r0, sub_rows), :], wbf[...], preferred_element_type=F32)
            gate = jnp.minimum(gu[:, :WEIGHT_TN] + bg, SWIGLU_LIMIT)
            up = jnp.clip(gu[:, WEIGHT_TN:] + bu, -SWIGLU_LIMIT, SWIGLU_LIMIT)
            hid[f, pl.ds(r0, sub_rows), :] = (
                (up + 1.0) * gate * jax.nn.sigmoid(SWIGLU_ALPHA * gate)).astype(BF16)
            return carry

        lax.fori_loop(0, n_sub, body, 0)

    def out_half(sub, bo):
        r0 = pl.multiple_of(sub * sub_rows, sub_rows)
        acc = bo
        for f in range(FF_PHASES):
            acc = acc + jnp.dot(hid[f, pl.ds(r0, sub_rows), :], wbf[f * WEIGHT_TN:(f + 1) * WEIGHT_TN, :],
                                preferred_element_type=F32)
        return r0, acc

    def out_phase_first(n_sub):
        bo = bout_ref[0:1, :]

        def body(sub, carry):
            r0, acc = out_half(sub, bo)
            ystash[pl.ds(r0, sub_rows), :] = acc
            return carry

        lax.fori_loop(0, n_sub, body, 0)

    def out_phase_second(n_sub):
        bo = bout_ref[1:2, :]
        half_chunks = TOKEN_CHUNKS // 2

        def body(sub, carry):
            r0, acc = out_half(sub, bo)
            ysl = sub % 2

            @pl.when(sub >= 2)
            def _():
                y_copy(w, sub, ysl).wait()

            stage = ybuf.at[ysl]
            for c in range(half_chunks):
                cs = slice(c * LANES, (c + 1) * LANES)
                stage[pl.ds(c, sub_rows, stride=TOKEN_CHUNKS), :] = ystash[pl.ds(r0, sub_rows), cs]
                stage[pl.ds(half_chunks + c, sub_rows, stride=TOKEN_CHUNKS), :] = acc[:, cs]
            y_copy(w, sub, ysl).start()
            return carry

        lax.fori_loop(0, n_sub, body, 0)
        for back in (1, 2):
            @pl.when(n_sub >= back)
            def _():
                y_copy(w, n_sub - back, (n_sub - back) % 2).wait()

    @pl.when(active)
    def _():
        @pl.when(w + 1 < n_work)
        def _():
            for_each_x_copy(w + 1, (w + 1) % 2, lambda cp: cp.start())

        for_each_x_copy(w, w % 2, lambda cp: cp.wait())
        n_sub = wn_ref[w]
        xsl = w % 2
        for p in range(n_phase):
            if p + 1 < n_phase:
                start_weights(w, p + 1)
            else:
                @pl.when(w + 1 < n_work)
                def _():
                    start_weights(w + 1, 0)

            for half in range(2):
                weight_copy(win_hbm, 0, 0, p % 2, half).wait()
                wbf[:, half * WEIGHT_TN:(half + 1) * WEIGHT_TN] = wbuf[p % 2, half].astype(BF16)
            if p < FF_PHASES:
                ff_phase(p, n_sub, xsl)
            elif p == FF_PHASES:
                out_phase_first(n_sub)
            else:
                out_phase_second(n_sub)


def _expert_ffn(xs, work, w_in, b_in, w_out, b_out, layer):
    n_rows, d = xs.shape
    ne = w_in.shape[1]
    w_expert, w_start, w_nsub, n_work, tail = work
    n_items = w_expert.shape[0]
    assert OUT_PHASES == 2
    bias_in = pl.BlockSpec((None, 2 * FF_PHASES, WEIGHT_TN), lambda w, we, *_: (we[w], 0, 0))
    bias_out = pl.BlockSpec((None, OUT_PHASES, 2 * WEIGHT_TN), lambda w, we, *_: (we[w], 0, 0))
    any_spec = pl.BlockSpec(memory_space=pl.ANY)
    return pl.pallas_call(
        functools.partial(_experts_kernel, layer=layer),
        grid_spec=pltpu.PrefetchScalarGridSpec(
            num_scalar_prefetch=5,
            grid=(n_items,),
            in_specs=[any_spec, any_spec, any_spec, bias_in, bias_out],
            out_specs=any_spec,
            scratch_shapes=[
                pltpu.VMEM((2, 2, d, WEIGHT_TN), F32),
                pltpu.VMEM((d, 2 * WEIGHT_TN), BF16),
                pltpu.VMEM((2, EXPERT_MAX_ROWS, d), BF16),
                pltpu.VMEM((FF_PHASES, EXPERT_MAX_ROWS, WEIGHT_TN), BF16),
                pltpu.VMEM((EXPERT_MAX_ROWS, 2 * WEIGHT_TN), F32),
                pltpu.VMEM((2, MOE_TM * TOKEN_CHUNKS, LANES), F32),
                pltpu.SemaphoreType.DMA((2, 2)),
                pltpu.SemaphoreType.DMA((2,)),
                pltpu.SemaphoreType.DMA((2,)),
            ],
        ),
        out_shape=jax.ShapeDtypeStruct((n_rows * TOKEN_CHUNKS, LANES), F32),
        compiler_params=_cparams(("arbitrary",)),
        name="expert_ffn",
    )(w_expert, w_start, w_nsub, n_work, tail, xs, w_in, w_out,
      b_in.reshape(ne, 2 * FF_PHASES, WEIGHT_TN), b_out.reshape(ne, OUT_PHASES, 2 * WEIGHT_TN))


def _combine_kernel(dest_ref, y_hbm, p_ref, x_ref, mod_ref, g_ref, b_ref, o_ref, buf, sem):
    i = pl.program_id(0)
    n_steps = pl.num_programs(0)

    def start_all(step, slot):
        def body(r, carry):
            for k in range(TOP_K):
                row = dest_ref[(step * COMBINE_TM + r) * TOP_K + k]
                _slab_copy(y_hbm, buf.at[slot, k], sem.at[slot], row, r).start(priority=k % 2)
            return carry
        lax.fori_loop(0, COMBINE_TM, body, 0, unroll=2)

    def wait_all(slot):
        def body(r, carry):
            for k in range(TOP_K):
                _slab_copy(y_hbm, buf.at[slot, k], sem.at[slot], 0, r).wait()
            return carry
        lax.fori_loop(0, COMBINE_TM, body, 0, unroll=2)

    @pl.when(i == 0)
    def _():
        start_all(0, 0)

    for slot in range(2):
        @pl.when((i % 2 == slot) & (i + 1 < n_steps))
        def _():
            start_all(i + 1, 1 - slot)

        @pl.when(i % 2 == slot)
        def _():
            wait_all(slot)
            p = p_ref[...]
            moe = p[:, 0:1] * _load_token_major(buf.at[slot, 0], COMBINE_TM, SLAB_PITCH)
            for k in range(1, TOP_K):
                moe = moe + p[:, k:k + 1] * _load_token_major(buf.at[slot, k], COMBINE_TM, SLAB_PITCH)
            gate = mod_ref[5:6, :]
            o_ref[...] = _layer_norm(DEEPNORM_ALPHA * x_ref[...] + (1.0 + gate) * moe, g_ref[...], b_ref[...])


def _combine_ln(y, dest, probs, x2, mod_l, ln_g, ln_b, seq):
    t, d = x2.shape
    tiles_per_seq = seq // COMBINE_TM
    return pl.pallas_call(
        _combine_kernel,
        grid_spec=pltpu.PrefetchScalarGridSpec(
            num_scalar_prefetch=1,
            grid=(t // COMBINE_TM,),
            in_specs=[
                pl.BlockSpec(memory_space=pl.ANY),
                pl.BlockSpec((COMBINE_TM, TOP_K), lambda i, dst: (i, 0)),
                pl.BlockSpec((COMBINE_TM, d), lambda i, dst: (i, 0)),
                pl.BlockSpec((None, N_MODULATIONS, d), lambda i, dst: (i // tiles_per_seq, 0, 0)),
                pl.BlockSpec((1, d), lambda i, dst: (0, 0)),
                pl.BlockSpec((1, d), lambda i, dst: (0, 0)),
            ],
            out_specs=pl.BlockSpec((COMBINE_TM, d), lambda i, dst: (i, 0)),
            scratch_shapes=[pltpu.VMEM((2, TOP_K, COMBINE_TM * SLAB_PITCH, LANES), F32),
                            pltpu.SemaphoreType.DMA((2,))],
        ),
        out_shape=jax.ShapeDtypeStruct((t, d), F32),
        compiler_params=_cparams(("arbitrary",)),
        name="combine_ln",
    )(dest, y, probs, x2, mod_l, ln_g.reshape(1, d), ln_b.reshape(1, d))


def _routing_tables(top_idx):
    t = top_idx.shape[0]
    flat_e = top_idx.reshape(-1)
    flat_t = jnp.repeat(jnp.arange(t, dtype=jnp.int32), TOP_K)
    onehot = jax.nn.one_hot(flat_e, N_EXPERTS, dtype=jnp.int32)
    counts = jnp.sum(onehot, axis=0)
    rank = jnp.take_along_axis(jnp.cumsum(onehot, axis=0) - onehot, flat_e[:, None], axis=1)[:, 0]
    padded = (counts + MOE_TM - 1) // MOE_TM * MOE_TM
    ends = jnp.cumsum(padded)
    starts = ends - padded
    dest = (starts[flat_e] + rank).astype(jnp.int32)
    n_rows = t * TOP_K + N_EXPERTS * MOE_TM
    row_tok = (jnp.arange(n_rows, dtype=jnp.int32) % t).at[dest].set(flat_t)
    chunks = (padded + EXPERT_MAX_ROWS - 1) // EXPERT_MAX_ROWS
    chunk_ends = jnp.cumsum(chunks)
    n_work = chunk_ends[-1:]
    item = jnp.arange(N_EXPERTS + n_rows // EXPERT_MAX_ROWS, dtype=jnp.int32)
    w_expert = jnp.minimum(jnp.sum(item[:, None] >= chunk_ends[None, :], axis=1), N_EXPERTS - 1)
    w_chunk = item - (chunk_ends[w_expert] - chunks[w_expert])
    live = item < n_work
    w_start = jnp.where(live, starts[w_expert] + w_chunk * EXPERT_MAX_ROWS, 0)
    w_rows = jnp.clip(padded[w_expert] - w_chunk * EXPERT_MAX_ROWS, 0, EXPERT_MAX_ROWS)
    w_nsub = jnp.where(live, w_rows // MOE_TM, 0)
    used_tiles = ends[-1] // MOE_TM
    tail = jnp.stack([used_tiles, n_rows // MOE_TM - used_tiles])
    work = tuple(a.astype(jnp.int32) for a in (w_expert, w_start, w_nsub, n_work, tail))
    return dest, row_tok, work


def _moe_block(h, logits, x2, mod_l, ln_g, ln_b, w_in, b_in, w_out, b_out, layer, seq):
    top_idx, probs = _route_topk(logits)
    dest, row_tok, work = _routing_tables(top_idx)
    xs = _moe_gather(h, row_tok)
    y = _expert_ffn(xs, work, w_in, b_in, w_out, b_out, layer)
    return _combine_ln(y, dest, probs, x2, mod_l, ln_g, ln_b, seq)


def kernel(x, c, positions, cond_w, cond_b, ln_g, ln_b, attn_w_qkv, attn_w_o, sg_w_in, sg_b_in, sg_ln_g, sg_ln_b, sg_w_spatial, sg_b_spatial, sg_w_out, router_w, router_b, expert_w_in, expert_b_in, expert_w_out, expert_b_out):
    bsz, seq, d = x.shape
    x2 = x.reshape(bsz * seq, d)
    mod = _modulation(c, cond_w, cond_b).reshape(DEPTH, bsz, N_MODULATIONS, d)
    tabs = _rope_tables(positions)

    w_qkv = attn_w_qkv[0].astype(BF16).reshape(d, 3, len(DILATED_GROUPS), GROUP_WIDTH)
    outs, lses = [], []
    for g, (_, dilation) in enumerate(DILATED_GROUPS):
        w_g = w_qkv[:, :, g, :].reshape(d, 3 * GROUP_WIDTH)
        qkv = _qkv_projection(x2, mod[0], w_g, tabs, bsz, seq, dilation)
        o, lse = _dilated_attention(qkv, dilation)
        outs.append(o)
        lses.append(lse)
    mixed = _merge_groups(outs, lses, seq)
    x2, h, logits = _proj_ln(mixed, attn_w_o[0].astype(BF16), x2, mod[0], ln_g[0, 0], ln_b[0, 0],
                             router_w[0], router_b[0], seq)
    x2 = _moe_block(h, logits, x2, mod[0], ln_g[0, 1], ln_b[0, 1], expert_w_in, expert_b_in[0],
                    expert_w_out, expert_b_out[0], 0, seq)

    z = _sg_in_projection(x2, mod[1], sg_w_in[0].astype(BF16), sg_b_in[0], seq)
    gated = _spatial_gating(z, sg_ln_g[0], sg_ln_b[0], sg_w_spatial[0], sg_b_spatial[0])
    x2, h, logits = _proj_ln(gated, sg_w_out[0].astype(BF16), x2, mod[1], ln_g[1, 0], ln_b[1, 0],
                             router_w[1], router_b[1], seq)
    x2 = _moe_block(h, logits, x2, mod[1], ln_g[1, 1], ln_b[1, 1], expert_w_in, expert_b_in[1],
                    expert_w_out, expert_b_out[1], 1, seq)
    return x2.reshape(bsz, seq, d)
```

```python
import functools

import jax
import jax.numpy as jnp
from jax import lax
from jax.experimental import pallas as pl
from jax.experimental.pallas import tpu as pltpu

F32 = jnp.float32
BF16 = jnp.bfloat16

D_MODEL = 2048
DEPTH = 2
HEAD_DIM = 64
HEADS_PER_GROUP = 16
DILATED_GROUPS = ((128, 1), (512, 4), (2048, 16))
GROUP_WIDTH = HEADS_PER_GROUP * HEAD_DIM
ATT_WIDTH = GROUP_WIDTH * len(DILATED_GROUPS)
ROPE_THETA = 500000.0
ROT_DIM = HEAD_DIM // 4
ATT_BLOCK = 128

SG_CHUNK = 128
SG_WIDTH = 2 * D_MODEL
SG_GROUPS = 16
SG_GROUP_DIM = SG_WIDTH // SG_GROUPS

N_EXPERTS = 32
TOP_K = 4
EXPERT_FF = D_MODEL
SWIGLU_LIMIT = 7.0
SWIGLU_ALPHA = 1.702

N_MODULATIONS = 6
DEEPNORM_ALPHA = (2 * DEPTH) ** 0.25
LN_EPS = 1e-5
NEG_BIG = -1e30

LANES = 128
TOKEN_CHUNKS = D_MODEL // LANES
SLAB_PITCH = TOKEN_CHUNKS + 1
VMEM_LIMIT_BYTES = 56 * 1024 * 1024

MOD_TN = 512
ROPE_TM = 1024
PROJ_TM = 1024
PROJ_TN = 1024
MERGE_TM = 512
LN_TM = 256
TOPK_TM = 1024
SGU_TM = 256
MOE_TM = 256
WEIGHT_TN = 512
FF_PHASES = EXPERT_FF // WEIGHT_TN
OUT_PHASES = D_MODEL // (2 * WEIGHT_TN)
EXPERT_MAX_ROWS = 5 * MOE_TM
GATHER_ROWS = 512
COMBINE_TM = 256


def _cparams(sem):
    return pltpu.CompilerParams(dimension_semantics=sem, vmem_limit_bytes=VMEM_LIMIT_BYTES)


def _store_token_major(ref, val):
    rows = val.shape[0]
    for c in range(TOKEN_CHUNKS):
        ref[pl.ds(c, rows, stride=TOKEN_CHUNKS), :] = val[:, c * LANES:(c + 1) * LANES]


def _load_token_major(ref, rows, pitch=TOKEN_CHUNKS):
    return jnp.concatenate(
        [ref[pl.ds(c, rows, stride=pitch), :] for c in range(TOKEN_CHUNKS)], axis=1)


def _layer_norm(z, g, b):
    mu = jnp.mean(z, axis=-1, keepdims=True)
    zc = z - mu
    var = jnp.mean(zc * zc, axis=-1, keepdims=True)
    return zc * lax.rsqrt(var + LN_EPS) * g + b


def _mod_kernel(ct_ref, w_ref, b_ref, o_ref):
    ct = ct_ref[...]
    ca = ct * jax.nn.sigmoid(ct)
    w = w_ref[...]
    for b in range(ct.shape[1]):
        o_ref[b:b + 1, :] = jnp.sum(ca[:, b:b + 1] * w, axis=0, keepdims=True) + b_ref[...]


def _modulation(c, cond_w, cond_b):
    nl, d, n = cond_w.shape
    bsz = c.shape[0]
    return pl.pallas_call(
        _mod_kernel,
        grid=(nl, n // MOD_TN),
        in_specs=[
            pl.BlockSpec((d, bsz), lambda l, j: (0, 0)),
            pl.BlockSpec((None, d, MOD_TN), lambda l, j: (l, 0, j)),
            pl.BlockSpec((None, 1, MOD_TN), lambda l, j: (l, 0, j)),
        ],
        out_specs=pl.BlockSpec((None, bsz, MOD_TN), lambda l, j: (l, 0, j)),
        out_shape=jax.ShapeDtypeStruct((nl, bsz, n), F32),
        compiler_params=_cparams(("parallel", "parallel")),
        name="modulation",
    )(c.T, cond_w, cond_b.reshape(nl, 1, n))


def _rope_kernel(pos_ref, inv_ref, c_ref, s1_ref, s2_ref):
    ang = pos_ref[...] * inv_ref[...]
    lane = lax.broadcasted_iota(jnp.int32, ang.shape, 1) & (HEAD_DIM - 1)
    cs = jnp.cos(ang)
    sn = jnp.sin(ang)
    half = ROT_DIM // 2
    c_ref[...] = jnp.where(lane < ROT_DIM, cs, 1.0)
    s1_ref[...] = jnp.where(lane < half, -sn, 0.0)
    s2_ref[...] = jnp.where((lane >= half) & (lane < ROT_DIM), sn, 0.0)


def _rope_tables(positions):
    t = positions.size
    pos = positions.astype(F32).reshape(t, 1)
    inv = jnp.power(jnp.float32(ROPE_THETA), -jnp.arange(0, ROT_DIM, 2, dtype=F32) / ROT_DIM)
    lane = jnp.arange(LANES) % HEAD_DIM
    inv_row = inv[lane % (ROT_DIM // 2)].reshape(1, LANES)
    spec = pl.BlockSpec((ROPE_TM, LANES), lambda i: (i, 0))
    shp = jax.ShapeDtypeStruct((t, LANES), F32)
    return pl.pallas_call(
        _rope_kernel,
        grid=(t // ROPE_TM,),
        in_specs=[pl.BlockSpec((ROPE_TM, 1), lambda i: (i, 0)), pl.BlockSpec((1, LANES), lambda i: (0, 0))],
        out_specs=[spec, spec, spec],
        out_shape=[shp, shp, shp],
        compiler_params=_cparams(("parallel",)),
        name="rope_tables",
    )(pos, inv_row)


def _modulate_into(h_scr, x_ref, mod_ref):
    shift = mod_ref[0:1, :]
    scale = mod_ref[1:2, :]
    h_scr[...] = (x_ref[...] * (1.0 + scale) + shift).astype(BF16)


def _qkv_kernel(x_ref, mod_ref, w_ref, c_ref, s1_ref, s2_ref, o_ref, h_scr, r_scr, *, dilation):
    j = pl.program_id(1)

    @pl.when(j == 0)
    def _():
        _modulate_into(h_scr, x_ref, mod_ref)

    acc = jnp.dot(h_scr[...], w_ref[...], preferred_element_type=F32)
    n_col_blocks = acc.shape[1] // LANES

    @pl.when(j < 2)
    def _():
        c = c_ref[...]
        s1 = s1_ref[...]
        s2 = s2_ref[...]
        for cb in range(n_col_blocks):
            a = acc[:, cb * LANES:(cb + 1) * LANES]
            up = pltpu.roll(a, LANES - ROT_DIM // 2, 1)
            dn = pltpu.roll(a, ROT_DIM // 2, 1)
            r_scr[cb] = a * c + up * s1 + dn * s2

    @pl.when(j == 2)
    def _():
        for cb in range(n_col_blocks):
            r_scr[cb] = acc[:, cb * LANES:(cb + 1) * LANES]

    rows = r_scr.shape[1] // dilation
    for r in range(dilation):
        for cb in range(n_col_blocks):
            o_ref[r, :, cb * LANES:(cb + 1) * LANES] = (
                r_scr[cb, pl.ds(r, rows, stride=dilation), :].astype(BF16))


def _qkv_projection(x2, mod_l, w_bf, tabs, bsz, seq, dilation):
    t, d = x2.shape
    tiles_per_seq = seq // PROJ_TM
    tab_spec = pl.BlockSpec((PROJ_TM, LANES), lambda i, j: (i, 0))
    return pl.pallas_call(
        functools.partial(_qkv_kernel, dilation=dilation),
        grid=(t // PROJ_TM, 3),
        in_specs=[
            pl.BlockSpec((PROJ_TM, d), lambda i, j: (i, 0)),
            pl.BlockSpec((None, N_MODULATIONS, d), lambda i, j: (i // tiles_per_seq, 0, 0)),
            pl.BlockSpec((d, GROUP_WIDTH), lambda i, j: (0, j)),
            tab_spec, tab_spec, tab_spec,
        ],
        out_specs=pl.BlockSpec((None, dilation, PROJ_TM // dilation, GROUP_WIDTH),
                               lambda i, j: (i // tiles_per_seq, 0, i % tiles_per_seq, j)),
        out_shape=jax.ShapeDtypeStruct((bsz, dilation, seq // dilation, 3 * GROUP_WIDTH), BF16),
        scratch_shapes=[pltpu.VMEM((PROJ_TM, d), BF16),
                        pltpu.VMEM((GROUP_WIDTH // LANES, PROJ_TM, LANES), F32)],
        compiler_params=_cparams(("arbitrary", "arbitrary")),
        name=f"qkv_projection_d{dilation}",
    )(x2, mod_l, w_bf, *tabs)


def _sg_in_kernel(x_ref, mod_ref, w_ref, b_ref, o_ref, h_scr):
    @pl.when(pl.program_id(1) == 0)
    def _():
        _modulate_into(h_scr, x_ref, mod_ref)

    z = jnp.dot(h_scr[...], w_ref[...], preferred_element_type=F32) + b_ref[...]
    o_ref[...] = (0.5 * z * (1.0 + lax.erf(z * (2.0 ** -0.5)))).astype(BF16)


def _sg_in_projection(x2, mod_l, w_bf, b_in, seq):
    t, d = x2.shape
    n = w_bf.shape[1]
    tiles_per_seq = seq // PROJ_TM
    return pl.pallas_call(
        _sg_in_kernel,
        grid=(t // PROJ_TM, n // PROJ_TN),
        in_specs=[
            pl.BlockSpec((PROJ_TM, d), lambda i, j: (i, 0)),
            pl.BlockSpec((None, N_MODULATIONS, d), lambda i, j: (i // tiles_per_seq, 0, 0)),
            pl.BlockSpec((d, PROJ_TN), lambda i, j: (0, j)),
            pl.BlockSpec((1, PROJ_TN), lambda i, j: (0, j)),
        ],
        out_specs=pl.BlockSpec((PROJ_TM, PROJ_TN), lambda i, j: (i, j)),
        out_shape=jax.ShapeDtypeStruct((t, n), BF16),
        scratch_shapes=[pltpu.VMEM((PROJ_TM, d), BF16)],
        compiler_params=_cparams(("arbitrary", "arbitrary")),
        name="sg_in_projection",
    )(x2, mod_l, w_bf, b_in.reshape(1, n))


def _attn_kernel(q_ref, kp_ref, kc_ref, vp_ref, vc_ref, o_ref, lse_ref, kcat, vcat, s_scr, p_scr, m_scr):
    n = pl.program_id(2)
    blk = q_ref.shape[0]
    n_pairs = GROUP_WIDTH // LANES
    lane = lax.broadcasted_iota(jnp.int32, (1, LANES), 1)
    keep = [jnp.where(lane < HEAD_DIM, 1.0, 0.0).astype(BF16), jnp.where(lane < HEAD_DIM, 0.0, 1.0).astype(BF16)]

    for pair in range(n_pairs):
        cs = slice(pair * LANES, (pair + 1) * LANES)
        for half in range(2):
            for j, (k_ref, v_ref) in enumerate(((kp_ref, vp_ref), (kc_ref, vc_ref))):
                rs = slice((2 * half + j) * blk, (2 * half + j + 1) * blk)
                kcat[pair, rs, :] = k_ref[:, cs] * keep[half]
                vcat[pair, rs, :] = v_ref[:, cs] * keep[half]

    dn = (((1,), (1,)), ((), ()))
    for pair in range(n_pairs):
        cs = slice(pair * LANES, (pair + 1) * LANES)
        q = q_ref[:, cs] * (HEAD_DIM ** -0.5)
        s_scr[pair] = lax.dot_general(q, kcat[pair], dn, preferred_element_type=F32)

    row = lax.broadcasted_iota(jnp.int32, (blk, 4 * blk), 0)
    col = lax.broadcasted_iota(jnp.int32, (blk, 4 * blk), 1)
    key = col & (blk - 1)
    is_prev = (col & blk) == 0
    ok = (is_prev & (key >= row) & (n > 0)) | (jnp.logical_not(is_prev) & (key <= row))
    first_head = col < 2 * blk
    low_full = lax.broadcasted_iota(jnp.int32, (blk, LANES), 1) < HEAD_DIM
    for pair in range(n_pairs):
        s = jnp.where(ok, s_scr[pair], NEG_BIG)
        m_a = jnp.max(s[:, :2 * blk], axis=-1, keepdims=True)
        m_b = jnp.max(s[:, 2 * blk:], axis=-1, keepdims=True)
        p_scr[pair] = jnp.exp(s - jnp.where(first_head, m_a, m_b)).astype(BF16)
        m_scr[pair] = jnp.where(low_full, m_a, m_b)

    r = lax.broadcasted_iota(jnp.int32, (4 * blk, LANES), 0)
    l = lax.broadcasted_iota(jnp.int32, (4 * blk, LANES), 1)
    head_sum = jnp.where((r < 2 * blk) == (l < HEAD_DIM), 1.0, 0.0).astype(BF16)
    for pair in range(n_pairs):
        cs = slice(pair * LANES, (pair + 1) * LANES)
        p = p_scr[pair]
        o = jnp.dot(p, vcat[pair], preferred_element_type=F32)
        den = jnp.dot(p, head_sum, preferred_element_type=F32)
        o_ref[:, cs] = (o / den).astype(BF16)
        lse_ref[:, cs] = m_scr[pair] + jnp.log(den)


def _dilated_attention(qkv, dilation):
    bsz, _, length, _ = qkv.shape
    nblk = length // ATT_BLOCK
    n_pairs = GROUP_WIDTH // LANES

    def spec(part, prev):
        def index(b, r, n):
            return (b, r, jnp.maximum(n - 1, 0) if prev else n, part)
        return pl.BlockSpec((None, None, ATT_BLOCK, GROUP_WIDTH), index)

    out_spec = pl.BlockSpec((None, None, ATT_BLOCK, GROUP_WIDTH), lambda b, r, n: (b, r, n, 0))
    return pl.pallas_call(
        _attn_kernel,
        grid=(bsz, dilation, nblk),
        in_specs=[spec(0, False), spec(1, True), spec(1, False), spec(2, True), spec(2, False)],
        out_specs=[out_spec, out_spec],
        out_shape=[jax.ShapeDtypeStruct((bsz, dilation, length, GROUP_WIDTH), BF16),
                   jax.ShapeDtypeStruct((bsz, dilation, length, GROUP_WIDTH), F32)],
        scratch_shapes=[
            pltpu.VMEM((n_pairs, 4 * ATT_BLOCK, LANES), BF16),
            pltpu.VMEM((n_pairs, 4 * ATT_BLOCK, LANES), BF16),
            pltpu.VMEM((n_pairs, ATT_BLOCK, 4 * ATT_BLOCK), F32),
            pltpu.VMEM((n_pairs, ATT_BLOCK, 4 * ATT_BLOCK), BF16),
            pltpu.VMEM((n_pairs, ATT_BLOCK, LANES), F32),
        ],
        compiler_params=_cparams(("parallel", "parallel", "parallel")),
        name=f"dilated_attention_d{dilation}",
    )(qkv, qkv, qkv, qkv, qkv)


def _merge_kernel(*refs):
    ng = len(DILATED_GROUPS)
    o_refs, l_refs, out_ref = refs[:ng], refs[ng:2 * ng], refs[2 * ng]
    scratch = refs[2 * ng + 1:]
    outs, lses = [], []
    for g, (_, dilation) in enumerate(DILATED_GROUPS):
        if dilation == 1:
            outs.append(o_refs[g][0].astype(F32))
            lses.append(l_refs[g][0])
            continue
        so, sl = scratch[2 * (g - 1)], scratch[2 * (g - 1) + 1]
        n_col_blocks = so.shape[0]
        rows = so.shape[1] // dilation
        for r in range(dilation):
            for cb in range(n_col_blocks):
                cs = slice(cb * LANES, (cb + 1) * LANES)
                so[cb, pl.ds(r, rows, stride=dilation), :] = o_refs[g][r, :, cs].astype(F32)
                sl[cb, pl.ds(r, rows, stride=dilation), :] = l_refs[g][r, :, cs]
        outs.append(jnp.concatenate([so[cb] for cb in range(n_col_blocks)], axis=1))
        lses.append(jnp.concatenate([sl[cb] for cb in range(n_col_blocks)], axis=1))
    mx = jnp.maximum(jnp.maximum(lses[0], lses[1]), lses[2])
    es = [jnp.exp(l - mx) for l in lses]
    inv = 1.0 / (es[0] + es[1] + es[2])
    for g in range(ng):
        out_ref[:, g * GROUP_WIDTH:(g + 1) * GROUP_WIDTH] = (outs[g] * (es[g] * inv)).astype(BF16)


def _merge_groups(outs, lses, seq):
    bsz = outs[0].shape[0]
    tiles_per_seq = seq // MERGE_TM
    specs = [pl.BlockSpec((None, dilation, MERGE_TM // dilation, GROUP_WIDTH),
                          lambda i: (i // tiles_per_seq, 0, i % tiles_per_seq, 0))
             for _, dilation in DILATED_GROUPS]
    assert DILATED_GROUPS[0][1] == 1
    scratch = []
    for _ in DILATED_GROUPS[1:]:
        scratch += [pltpu.VMEM((GROUP_WIDTH // LANES, MERGE_TM, LANES), F32)] * 2
    return pl.pallas_call(
        _merge_kernel,
        grid=(bsz * tiles_per_seq,),
        in_specs=specs + specs,
        out_specs=pl.BlockSpec((MERGE_TM, ATT_WIDTH), lambda i: (i, 0)),
        out_shape=jax.ShapeDtypeStruct((bsz * seq, ATT_WIDTH), BF16),
        scratch_shapes=scratch,
        compiler_params=_cparams(("parallel",)),
        name="merge_groups",
    )(*outs, *lses)


def _sgu_kernel(u_ref, v_ref, g_ref, b_ref, wsp_ref, bsp_ref, o_ref):
    v = v_ref[...].astype(F32)
    vn = _layer_norm(v, g_ref[...], b_ref[...]).astype(BF16)
    row = lax.broadcasted_iota(jnp.int32, (SG_CHUNK, SG_CHUNK), 0)
    col = lax.broadcasted_iota(jnp.int32, (SG_CHUNK, SG_CHUNK), 1)
    causal = col <= row
    for g in range(SG_GROUPS):
        w = jnp.where(causal, wsp_ref[g], 0.0).astype(BF16)
        bias = bsp_ref[:, g:g + 1]
        cs = slice(g * SG_GROUP_DIM, (g + 1) * SG_GROUP_DIM)
        for c in range(u_ref.shape[0] // SG_CHUNK):
            rs = slice(c * SG_CHUNK, (c + 1) * SG_CHUNK)
            mixed = jnp.dot(w, vn[rs, cs], preferred_element_type=F32) + bias
            o_ref[rs, cs] = (u_ref[rs, cs].astype(F32) * mixed).astype(BF16)


def _spatial_gating(z, ln_g, ln_b, w_sp, b_sp):
    t = z.shape[0]
    return pl.pallas_call(
        _sgu_kernel,
        grid=(t // SGU_TM,),
        in_specs=[
            pl.BlockSpec((SGU_TM, SG_WIDTH), lambda i: (i, 0)),
            pl.BlockSpec((SGU_TM, SG_WIDTH), lambda i: (i, 1)),
            pl.BlockSpec((1, SG_WIDTH), lambda i: (0, 0)),
            pl.BlockSpec((1, SG_WIDTH), lambda i: (0, 0)),
            pl.BlockSpec((SG_GROUPS, SG_CHUNK, SG_CHUNK), lambda i: (0, 0, 0)),
            pl.BlockSpec((SG_CHUNK, SG_GROUPS), lambda i: (0, 0)),
        ],
        out_specs=pl.BlockSpec((SGU_TM, SG_WIDTH), lambda i: (i, 0)),
        out_shape=jax.ShapeDtypeStruct((t, SG_WIDTH), BF16),
        compiler_params=_cparams(("parallel",)),
        name="spatial_gating",
    )(z, z, ln_g.reshape(1, -1), ln_b.reshape(1, -1), w_sp, b_sp.T)


def _proj_ln_kernel(a_ref, w_ref, x_ref, mod_ref, g_ref, b_ref, rw_hi_ref, rw_lo_ref, rb_ref,
                    x_out, h_out, logit_out):
    y = jnp.dot(a_ref[...], w_ref[...], preferred_element_type=F32)
    gate = mod_ref[2:3, :]
    xn = _layer_norm(DEEPNORM_ALPHA * x_ref[...] + (1.0 + gate) * y, g_ref[...], b_ref[...])
    x_out[...] = xn
    h = xn * (1.0 + mod_ref[4:5, :]) + mod_ref[3:4, :]
    _store_token_major(h_out, h)
    h_hi = h.astype(BF16)
    h_lo = (h - h_hi.astype(F32)).astype(BF16)
    rw_hi = rw_hi_ref[...]
    logits = jnp.dot(h_hi, rw_hi, preferred_element_type=F32)
    logits = logits + jnp.dot(h_hi, rw_lo_ref[...], preferred_element_type=F32)
    logits = logits + jnp.dot(h_lo, rw_hi, preferred_element_type=F32)
    logit_out[...] = logits + rb_ref[...]


def _proj_ln(a, w_bf, x2, mod_l, ln_g, ln_b, router_w, router_b, seq):
    t, d = x2.shape
    k = a.shape[1]
    tiles_per_seq = seq // LN_TM
    rw_hi = router_w.astype(BF16)
    rw_lo = (router_w - rw_hi.astype(F32)).astype(BF16)
    row = lambda i: (i, 0)
    fixed = lambda i: (0, 0)
    return pl.pallas_call(
        _proj_ln_kernel,
        grid=(t // LN_TM,),
        in_specs=[
            pl.BlockSpec((LN_TM, k), row),
            pl.BlockSpec((k, d), fixed, pipeline_mode=pl.Buffered(1)),
            pl.BlockSpec((LN_TM, d), row),
            pl.BlockSpec((None, N_MODULATIONS, d), lambda i: (i // tiles_per_seq, 0, 0)),
            pl.BlockSpec((1, d), fixed),
            pl.BlockSpec((1, d), fixed),
            pl.BlockSpec((d, N_EXPERTS), fixed),
            pl.BlockSpec((d, N_EXPERTS), fixed),
            pl.BlockSpec((1, N_EXPERTS), fixed),
        ],
        out_specs=[pl.BlockSpec((LN_TM, d), row), pl.BlockSpec((LN_TM * TOKEN_CHUNKS, LANES), row),
                   pl.BlockSpec((LN_TM, N_EXPERTS), row)],
        out_shape=[jax.ShapeDtypeStruct((t, d), F32), jax.ShapeDtypeStruct((t * TOKEN_CHUNKS, LANES), F32),
                   jax.ShapeDtypeStruct((t, N_EXPERTS), F32)],
        compiler_params=_cparams(("parallel",)),
        name="proj_ln",
    )(a, w_bf, x2, mod_l, ln_g.reshape(1, d), ln_b.reshape(1, d), rw_hi, rw_lo,
      router_b.reshape(1, N_EXPERTS))


def _topk_kernel(l_ref, idx_ref, p_ref):
    l = l_ref[...]
    tm = l.shape[0]
    lane = lax.broadcasted_iota(jnp.int32, l.shape, 1)
    out_lane = lax.broadcasted_iota(jnp.int32, (tm, TOP_K), 1)
    idx_out = jnp.zeros((tm, TOP_K), jnp.int32)
    val_out = jnp.zeros((tm, TOP_K), F32)
    for k in range(TOP_K):
        m = jnp.max(l, axis=-1, keepdims=True)
        idx = jnp.min(jnp.where(l == m, lane, N_EXPERTS), axis=-1, keepdims=True)
        idx_out = jnp.where(out_lane == k, idx, idx_out)
        val_out = jnp.where(out_lane == k, m, val_out)
        l = jnp.where(lane == idx, -jnp.inf, l)
    e = jnp.exp(val_out - jnp.max(val_out, axis=-1, keepdims=True))
    idx_ref[...] = idx_out
    p_ref[...] = e / jnp.sum(e, axis=-1, keepdims=True)


def _route_topk(logits):
    t = logits.shape[0]
    return pl.pallas_call(
        _topk_kernel,
        grid=(t // TOPK_TM,),
        in_specs=[pl.BlockSpec((TOPK_TM, N_EXPERTS), lambda i: (i, 0))],
        out_specs=[pl.BlockSpec((TOPK_TM, TOP_K), lambda i: (i, 0))] * 2,
        out_shape=[jax.ShapeDtypeStruct((t, TOP_K), jnp.int32), jax.ShapeDtypeStruct((t, TOP_K), F32)],
        compiler_params=_cparams(("parallel",)),
        name="route_topk",
    )(logits)


def _slab_copy(src_hbm, dst_vmem, sem, src_tok, dst_tok):
    src = pl.multiple_of(src_tok * TOKEN_CHUNKS, TOKEN_CHUNKS)
    return pltpu.make_async_copy(src_hbm.at[pl.ds(src, TOKEN_CHUNKS)],
                                 dst_vmem.at[pl.ds(dst_tok * SLAB_PITCH, TOKEN_CHUNKS)], sem)


def _gather_kernel(tok_ref, h_hbm, o_ref, buf, sem):
    i = pl.program_id(0)
    n_steps = pl.num_programs(0)

    def start_all(step, slot):
        def body(r2, carry):
            for prio in range(2):
                r = 2 * r2 + prio
                _slab_copy(h_hbm, buf.at[slot], sem.at[slot], tok_ref[step * GATHER_ROWS + r], r).start(
                    priority=prio)
            return carry
        lax.fori_loop(0, GATHER_ROWS // 2, body, 0, unroll=4)

    def wait_all(slot):
        def body(r, carry):
            _slab_copy(h_hbm, buf.at[slot], sem.at[slot], 0, r).wait()
            return carry
        lax.fori_loop(0, GATHER_ROWS, body, 0, unroll=8)

    @pl.when(i == 0)
    def _():
        start_all(0, 0)

    for slot in range(2):
        @pl.when((i % 2 == slot) & (i + 1 < n_steps))
        def _():
            start_all(i + 1, 1 - slot)

        @pl.when(i % 2 == slot)
        def _():
            wait_all(slot)
            o_ref[...] = _load_token_major(buf.at[slot], GATHER_ROWS, SLAB_PITCH).astype(BF16)


def _moe_gather(h_tm, row_tok):
    n_rows = row_tok.shape[0]
    return pl.pallas_call(
        _gather_kernel,
        grid_spec=pltpu.PrefetchScalarGridSpec(
            num_scalar_prefetch=1,
            grid=(n_rows // GATHER_ROWS,),
            in_specs=[pl.BlockSpec(memory_space=pl.ANY)],
            out_specs=pl.BlockSpec((GATHER_ROWS, D_MODEL), lambda i, tok: (i, 0)),
            scratch_shapes=[pltpu.VMEM((2, GATHER_ROWS * SLAB_PITCH, LANES), F32),
                            pltpu.SemaphoreType.DMA((2,))],
        ),
        out_shape=jax.ShapeDtypeStruct((n_rows, D_MODEL), BF16),
        compiler_params=_cparams(("arbitrary",)),
        name="moe_gather",
    )(row_tok, h_tm)


def _experts_kernel(we_ref, ws_ref, wn_ref, nw_ref, tail_ref, xs_hbm, win_hbm, wout_hbm, bin_ref, bout_ref, y_hbm,
                    wbuf, wbf, xbuf, hid, ystash, ybuf, wsem, xsem, ysem, *, layer):
    w = pl.program_id(0)
    n_phase = FF_PHASES + OUT_PHASES
    assert n_phase % 2 == 0
    n_work = nw_ref[0]
    active = w < n_work
    sub_rows = MOE_TM

    k_half = wbuf.shape[2] // 2

    def weight_copy(src_hbm, expert, col_tile, sl, half, part):
        rows = pl.ds(part * k_half, k_half)
        return pltpu.make_async_copy(src_hbm.at[layer, expert, rows, pl.ds(col_tile * WEIGHT_TN, WEIGHT_TN)],
                                     wbuf.at[sl, half, rows], wsem.at[sl, half, part])

    def start_weights(item, phase):
        expert = we_ref[item]
        sl = phase % 2
        if phase < FF_PHASES:
            src, tiles = win_hbm, (phase, FF_PHASES + phase)
        else:
            src, tiles = wout_hbm, (2 * (phase - FF_PHASES), 2 * (phase - FF_PHASES) + 1)
        for half in range(2):
            for part in range(2):
                weight_copy(src, expert, tiles[half], sl, half, part).start(priority=part)

    def x_copy(item, sub, xsl):
        row = pl.multiple_of(ws_ref[item] + sub * sub_rows, sub_rows)
        return pltpu.make_async_copy(xs_hbm.at[pl.ds(row, sub_rows)],
                                     xbuf.at[xsl, pl.ds(sub * sub_rows, sub_rows)], xsem.at[xsl])

    def for_each_x_copy(item, xsl, fn):
        for sub in range(EXPERT_MAX_ROWS // sub_rows):
            @pl.when(sub < wn_ref[item])
            def _():
                fn(x_copy(item, sub, xsl))

    def y_copy(item, sub, ysl):
        row = pl.multiple_of((ws_ref[item] + sub * sub_rows) * TOKEN_CHUNKS, sub_rows * TOKEN_CHUNKS)
        return pltpu.make_async_copy(ybuf.at[ysl], y_hbm.at[pl.ds(row, sub_rows * TOKEN_CHUNKS)], ysem.at[ysl])

    @pl.when(w == 0)
    def _():
        start_weights(0, 0)
        for_each_x_copy(0, 0, lambda cp: cp.start())
        ybuf[0] = jnp.zeros(ybuf.shape[1:], F32)

        def tail_copy(i):
            row = pl.multiple_of((tail_ref[0] + i) * (sub_rows * TOKEN_CHUNKS), sub_rows * TOKEN_CHUNKS)
            return pltpu.make_async_copy(ybuf.at[0], y_hbm.at[pl.ds(row, sub_rows * TOKEN_CHUNKS)], ysem.at[0])

        def start_tail(i, carry):
            tail_copy(i).start()
            return carry

        def wait_tail(i, carry):
            tail_copy(i).wait()
            return carry

        lax.fori_loop(0, tail_ref[1], start_tail, 0)
        lax.fori_loop(0, tail_ref[1], wait_tail, 0)

    def ff_phase(f, n_sub, xsl):
        bg = bin_ref[f:f + 1, :]
        bu = bin_ref[FF_PHASES + f:FF_PHASES + f + 1, :]

        def body(sub, carry):
            r0 = pl.multiple_of(sub * sub_rows, sub_rows)
            gu = jnp.dot(xbuf[xsl, pl.ds(r0, sub_rows), :], wbf[...], preferred_element_type=F32)
            gate = jnp.minimum(gu[:, :WEIGHT_TN] + bg, SWIGLU_LIMIT)
            up = jnp.clip(gu[:, WEIGHT_TN:] + bu, -SWIGLU_LIMIT, SWIGLU_LIMIT)
            hid[f, pl.ds(r0, sub_rows), :] = (
                (up + 1.0) * gate * jax.nn.sigmoid(SWIGLU_ALPHA * gate)).astype(BF16)
            return carry

        lax.fori_loop(0, n_sub, body, 0)

    def out_half(sub, bo):
        r0 = pl.multiple_of(sub * sub_rows, sub_rows)
        acc = bo
        for f in range(FF_PHASES):
            acc = acc + jnp.dot(hid[f, pl.ds(r0, sub_rows), :], wbf[f * WEIGHT_TN:(f + 1) * WEIGHT_TN, :],
                                preferred_element_type=F32)
        return r0, acc

    def out_phase_first(n_sub):
        bo = bout_ref[0:1, :]

        def body(sub, carry):
            r0, acc = out_half(sub, bo)
            ystash[pl.ds(r0, sub_rows), :] = acc
            return carry

        lax.fori_loop(0, n_sub, body, 0)

    def out_phase_second(n_sub):
        bo = bout_ref[1:2, :]
        half_chunks = TOKEN_CHUNKS // 2

        def body(sub, carry):
            r0, acc = out_half(sub, bo)
            ysl = sub % 2

            @pl.when(sub >= 2)
            def _():
                y_copy(w, sub, ysl).wait()

            stage = ybuf.at[ysl]
            for c in range(half_chunks):
                cs = slice(c * LANES, (c + 1) * LANES)
                stage[pl.ds(c, sub_rows, stride=TOKEN_CHUNKS), :] = ystash[pl.ds(r0, sub_rows), cs]
                stage[pl.ds(half_chunks + c, sub_rows, stride=TOKEN_CHUNKS), :] = acc[:, cs]
            y_copy(w, sub, ysl).start()
            return carry

        lax.fori_loop(0, n_sub, body, 0)
        for back in (1, 2):
            @pl.when(n_sub >= back)
            def _():
                y_copy(w, n_sub - back, (n_sub - back) % 2).wait()

    @pl.when(active)
    def _():
        @pl.when(w + 1 < n_work)
        def _():
            for_each_x_copy(w + 1, (w + 1) % 2, lambda cp: cp.start())

        for_each_x_copy(w, w % 2, lambda cp: cp.wait())
        n_sub = wn_ref[w]
        xsl = w % 2
        for p in range(n_phase):
            if p + 1 < n_phase:
                start_weights(w, p + 1)
            else:
                @pl.when(w + 1 < n_work)
                def _():
                    start_weights(w + 1, 0)

            for half in range(2):
                for part in range(2):
                    weight_copy(win_hbm, 0, 0, p % 2, half, part).wait()
                wbf[:, half * WEIGHT_TN:(half + 1) * WEIGHT_TN] = wbuf[p % 2, half].astype(BF16)
            if p < FF_PHASES:
                ff_phase(p, n_sub, xsl)
            elif p == FF_PHASES:
                out_phase_first(n_sub)
            else:
                out_phase_second(n_sub)


def _expert_ffn(xs, work, w_in, b_in, w_out, b_out, layer):
    n_rows, d = xs.shape
    ne = w_in.shape[1]
    w_expert, w_start, w_nsub, n_work, tail = work
    n_items = w_expert.shape[0]
    assert OUT_PHASES == 2
    bias_in = pl.BlockSpec((None, 2 * FF_PHASES, WEIGHT_TN), lambda w, we, *_: (we[w], 0, 0))
    bias_out = pl.BlockSpec((None, OUT_PHASES, 2 * WEIGHT_TN), lambda w, we, *_: (we[w], 0, 0))
    any_spec = pl.BlockSpec(memory_space=pl.ANY)
    return pl.pallas_call(
        functools.partial(_experts_kernel, layer=layer),
        grid_spec=pltpu.PrefetchScalarGridSpec(
            num_scalar_prefetch=5,
            grid=(n_items,),
            in_specs=[any_spec, any_spec, any_spec, bias_in, bias_out],
            out_specs=any_spec,
            scratch_shapes=[
                pltpu.VMEM((2, 2, d, WEIGHT_TN), F32),
                pltpu.VMEM((d, 2 * WEIGHT_TN), BF16),
                pltpu.VMEM((2, EXPERT_MAX_ROWS, d), BF16),
                pltpu.VMEM((FF_PHASES, EXPERT_MAX_ROWS, WEIGHT_TN), BF16),
                pltpu.VMEM((EXPERT_MAX_ROWS, 2 * WEIGHT_TN), F32),
                pltpu.VMEM((2, MOE_TM * TOKEN_CHUNKS, LANES), F32),
                pltpu.SemaphoreType.DMA((2, 2, 2)),
                pltpu.SemaphoreType.DMA((2,)),
                pltpu.SemaphoreType.DMA((2,)),
            ],
        ),
        out_shape=jax.ShapeDtypeStruct((n_rows * TOKEN_CHUNKS, LANES), F32),
        compiler_params=_cparams(("arbitrary",)),
        name="expert_ffn",
    )(w_expert, w_start, w_nsub, n_work, tail, xs, w_in, w_out,
      b_in.reshape(ne, 2 * FF_PHASES, WEIGHT_TN), b_out.reshape(ne, OUT_PHASES, 2 * WEIGHT_TN))


def _combine_kernel(dest_ref, y_hbm, p_ref, x_ref, mod_ref, g_ref, b_ref, o_ref, buf, sem):
    i = pl.program_id(0)
    n_steps = pl.num_programs(0)

    def start_all(step, slot):
        def body(r, carry):
            for k in range(TOP_K):
                row = dest_ref[(step * COMBINE_TM + r) * TOP_K + k]
                _slab_copy(y_hbm, buf.at[slot, k], sem.at[slot], row, r).start(priority=k % 2)
            return carry
        lax.fori_loop(0, COMBINE_TM, body, 0, unroll=2)

    def wait_all(slot):
        def body(r, carry):
            for k in range(TOP_K):
                _slab_copy(y_hbm, buf.at[slot, k], sem.at[slot], 0, r).wait()
            return carry
        lax.fori_loop(0, COMBINE_TM, body, 0, unroll=2)

    @pl.when(i == 0)
    def _():
        start_all(0, 0)

    for slot in range(2):
        @pl.when((i % 2 == slot) & (i + 1 < n_steps))
        def _():
            start_all(i + 1, 1 - slot)

        @pl.when(i % 2 == slot)
        def _():
            wait_all(slot)
            p = p_ref[...]
            moe = p[:, 0:1] * _load_token_major(buf.at[slot, 0], COMBINE_TM, SLAB_PITCH)
            for k in range(1, TOP_K):
                moe = moe + p[:, k:k + 1] * _load_token_major(buf.at[slot, k], COMBINE_TM, SLAB_PITCH)
            gate = mod_ref[5:6, :]
            o_ref[...] = _layer_norm(DEEPNORM_ALPHA * x_ref[...] + (1.0 + gate) * moe, g_ref[...], b_ref[...])


def _combine_ln(y, dest, probs, x2, mod_l, ln_g, ln_b, seq):
    t, d = x2.shape
    tiles_per_seq = seq // COMBINE_TM
    return pl.pallas_call(
        _combine_kernel,
        grid_spec=pltpu.PrefetchScalarGridSpec(
            num_scalar_prefetch=1,
            grid=(t // COMBINE_TM,),
            in_specs=[
                pl.BlockSpec(memory_space=pl.ANY),
                pl.BlockSpec((COMBINE_TM, TOP_K), lambda i, dst: (i, 0)),
                pl.BlockSpec((COMBINE_TM, d), lambda i, dst: (i, 0)),
                pl.BlockSpec((None, N_MODULATIONS, d), lambda i, dst: (i // tiles_per_seq, 0, 0)),
                pl.BlockSpec((1, d), lambda i, dst: (0, 0)),
                pl.BlockSpec((1, d), lambda i, dst: (0, 0)),
            ],
            out_specs=pl.BlockSpec((COMBINE_TM, d), lambda i, dst: (i, 0)),
            scratch_shapes=[pltpu.VMEM((2, TOP_K, COMBINE_TM * SLAB_PITCH, LANES), F32),
                            pltpu.SemaphoreType.DMA((2,))],
        ),
        out_shape=jax.ShapeDtypeStruct((t, d), F32),
        compiler_params=_cparams(("arbitrary",)),
        name="combine_ln",
    )(dest, y, probs, x2, mod_l, ln_g.reshape(1, d), ln_b.reshape(1, d))


def _routing_tables(top_idx):
    t = top_idx.shape[0]
    flat_e = top_idx.reshape(-1)
    flat_t = jnp.repeat(jnp.arange(t, dtype=jnp.int32), TOP_K)
    onehot = jax.nn.one_hot(flat_e, N_EXPERTS, dtype=jnp.int32)
    counts = jnp.sum(onehot, axis=0)
    rank = jnp.take_along_axis(jnp.cumsum(onehot, axis=0) - onehot, flat_e[:, None], axis=1)[:, 0]
    padded = (counts + MOE_TM - 1) // MOE_TM * MOE_TM
    ends = jnp.cumsum(padded)
    starts = ends - padded
    dest = (starts[flat_e] + rank).astype(jnp.int32)
    n_rows = t * TOP_K + N_EXPERTS * MOE_TM
    row_tok = (jnp.arange(n_rows, dtype=jnp.int32) % t).at[dest].set(flat_t)
    chunks = (padded + EXPERT_MAX_ROWS - 1) // EXPERT_MAX_ROWS
    chunk_ends = jnp.cumsum(chunks)
    n_work = chunk_ends[-1:]
    item = jnp.arange(N_EXPERTS + n_rows // EXPERT_MAX_ROWS, dtype=jnp.int32)
    w_expert = jnp.minimum(jnp.sum(item[:, None] >= chunk_ends[None, :], axis=1), N_EXPERTS - 1)
    w_chunk = item - (chunk_ends[w_expert] - chunks[w_expert])
    live = item < n_work
    w_start = jnp.where(live, starts[w_expert] + w_chunk * EXPERT_MAX_ROWS, 0)
    w_rows = jnp.clip(padded[w_expert] - w_chunk * EXPERT_MAX_ROWS, 0, EXPERT_MAX_ROWS)
    w_nsub = jnp.where(live, w_rows // MOE_TM, 0)
    used_tiles = ends[-1] // MOE_TM
    tail = jnp.stack([used_tiles, n_rows // MOE_TM - used_tiles])
    work = tuple(a.astype(jnp.int32) for a in (w_expert, w_start, w_nsub, n_work, tail))
    return dest, row_tok, work


def _moe_block(h, logits, x2, mod_l, ln_g, ln_b, w_in, b_in, w_out, b_out, layer, seq):
    top_idx, probs = _route_topk(logits)
    dest, row_tok, work = _routing_tables(top_idx)
    xs = _moe_gather(h, row_tok)
    y = _expert_ffn(xs, work, w_in, b_in, w_out, b_out, layer)
    return _combine_ln(y, dest, probs, x2, mod_l, ln_g, ln_b, seq)


def kernel(x, c, positions, cond_w, cond_b, ln_g, ln_b, attn_w_qkv, attn_w_o, sg_w_in, sg_b_in, sg_ln_g, sg_ln_b, sg_w_spatial, sg_b_spatial, sg_w_out, router_w, router_b, expert_w_in, expert_b_in, expert_w_out, expert_b_out):
    bsz, seq, d = x.shape
    x2 = x.reshape(bsz * seq, d)
    mod = _modulation(c, cond_w, cond_b).reshape(DEPTH, bsz, N_MODULATIONS, d)
    tabs = _rope_tables(positions)

    w_qkv = attn_w_qkv[0].astype(BF16).reshape(d, 3, len(DILATED_GROUPS), GROUP_WIDTH)
    outs, lses = [], []
    for g, (_, dilation) in enumerate(DILATED_GROUPS):
        w_g = w_qkv[:, :, g, :].reshape(d, 3 * GROUP_WIDTH)
        qkv = _qkv_projection(x2, mod[0], w_g, tabs, bsz, seq, dilation)
        o, lse = _dilated_attention(qkv, dilation)
        outs.append(o)
        lses.append(lse)
    mixed = _merge_groups(outs, lses, seq)
    x2, h, logits = _proj_ln(mixed, attn_w_o[0].astype(BF16), x2, mod[0], ln_g[0, 0], ln_b[0, 0],
                             router_w[0], router_b[0], seq)
    x2 = _moe_block(h, logits, x2, mod[0], ln_g[0, 1], ln_b[0, 1], expert_w_in, expert_b_in[0],
                    expert_w_out, expert_b_out[0], 0, seq)

    z = _sg_in_projection(x2, mod[1], sg_w_in[0].astype(BF16), sg_b_in[0], seq)
    gated = _spatial_gating(z, sg_ln_g[0], sg_ln_b[0], sg_w_spatial[0], sg_b_spatial[0])
    x2, h, logits = _proj_ln(gated, sg_w_out[0].astype(BF16), x2, mod[1], ln_g[1, 0], ln_b[1, 0],
                             router_w[1], router_b[1], seq)
    x2 = _moe_block(h, logits, x2, mod[1], ln_g[1, 1], ln_b[1, 1], expert_w_in, expert_b_in[1],
                    expert_w_out, expert_b_out[1], 1, seq)
    return x2.reshape(bsz, seq, d)
```

```python
import functools

import jax
import jax.numpy as jnp
from jax import lax
from jax.experimental import pallas as pl
from jax.experimental.pallas import tpu as pltpu

F32 = jnp.float32
BF16 = jnp.bfloat16

D_MODEL = 2048
DEPTH = 2
HEAD_DIM = 64
HEADS_PER_GROUP = 16
DILATED_GROUPS = ((128, 1), (512, 4), (2048, 16))
GROUP_WIDTH = HEADS_PER_GROUP * HEAD_DIM
ATT_WIDTH = GROUP_WIDTH * len(DILATED_GROUPS)
ROPE_THETA = 500000.0
ROT_DIM = HEAD_DIM // 4
ATT_BLOCK = 128

SG_CHUNK = 128
SG_WIDTH = 2 * D_MODEL
SG_GROUPS = 16
SG_GROUP_DIM = SG_WIDTH // SG_GROUPS

N_EXPERTS = 32
TOP_K = 4
EXPERT_FF = D_MODEL
SWIGLU_LIMIT = 7.0
SWIGLU_ALPHA = 1.702

N_MODULATIONS = 6
DEEPNORM_ALPHA = (2 * DEPTH) ** 0.25
LN_EPS = 1e-5
NEG_BIG = -1e30

LANES = 128
TOKEN_CHUNKS = D_MODEL // LANES
SLAB_PITCH = TOKEN_CHUNKS + 1
VMEM_LIMIT_BYTES = 56 * 1024 * 1024

MOD_TN = 512
ROPE_TM = 1024
PROJ_TM = 1024
PROJ_TN = 1024
MERGE_TM = 512
LN_TM = 256
TOPK_TM = 1024
SGU_TM = 256
MOE_TM = 256
MOE_PAD = MOE_TM // 2
WEIGHT_TN = 512
FF_PHASES = EXPERT_FF // WEIGHT_TN
OUT_PHASES = D_MODEL // (2 * WEIGHT_TN)
EXPERT_MAX_ROWS = 5 * MOE_TM
GATHER_ROWS = 512
COMBINE_TM = 256


def _cparams(sem):
    return pltpu.CompilerParams(dimension_semantics=sem, vmem_limit_bytes=VMEM_LIMIT_BYTES)


def _store_token_major(ref, val):
    rows = val.shape[0]
    for c in range(TOKEN_CHUNKS):
        ref[pl.ds(c, rows, stride=TOKEN_CHUNKS), :] = val[:, c * LANES:(c + 1) * LANES]


def _load_token_major(ref, rows, pitch=TOKEN_CHUNKS):
    return jnp.concatenate(
        [ref[pl.ds(c, rows, stride=pitch), :] for c in range(TOKEN_CHUNKS)], axis=1)


def _layer_norm(z, g, b):
    mu = jnp.mean(z, axis=-1, keepdims=True)
    zc = z - mu
    var = jnp.mean(zc * zc, axis=-1, keepdims=True)
    return zc * lax.rsqrt(var + LN_EPS) * g + b


def _mod_kernel(ct_ref, w_ref, b_ref, o_ref):
    ct = ct_ref[...]
    ca = ct * jax.nn.sigmoid(ct)
    w = w_ref[...]
    for b in range(ct.shape[1]):
        o_ref[b:b + 1, :] = jnp.sum(ca[:, b:b + 1] * w, axis=0, keepdims=True) + b_ref[...]


def _modulation(c, cond_w, cond_b):
    nl, d, n = cond_w.shape
    bsz = c.shape[0]
    return pl.pallas_call(
        _mod_kernel,
        grid=(nl, n // MOD_TN),
        in_specs=[
            pl.BlockSpec((d, bsz), lambda l, j: (0, 0)),
            pl.BlockSpec((None, d, MOD_TN), lambda l, j: (l, 0, j)),
            pl.BlockSpec((None, 1, MOD_TN), lambda l, j: (l, 0, j)),
        ],
        out_specs=pl.BlockSpec((None, bsz, MOD_TN), lambda l, j: (l, 0, j)),
        out_shape=jax.ShapeDtypeStruct((nl, bsz, n), F32),
        compiler_params=_cparams(("parallel", "parallel")),
        name="modulation",
    )(c.T, cond_w, cond_b.reshape(nl, 1, n))


def _rope_kernel(pos_ref, inv_ref, c_ref, s1_ref, s2_ref):
    ang = pos_ref[...] * inv_ref[...]
    lane = lax.broadcasted_iota(jnp.int32, ang.shape, 1) & (HEAD_DIM - 1)
    cs = jnp.cos(ang)
    sn = jnp.sin(ang)
    half = ROT_DIM // 2
    c_ref[...] = jnp.where(lane < ROT_DIM, cs, 1.0)
    s1_ref[...] = jnp.where(lane < half, -sn, 0.0)
    s2_ref[...] = jnp.where((lane >= half) & (lane < ROT_DIM), sn, 0.0)


def _rope_tables(positions):
    t = positions.size
    pos = positions.astype(F32).reshape(t, 1)
    inv = jnp.power(jnp.float32(ROPE_THETA), -jnp.arange(0, ROT_DIM, 2, dtype=F32) / ROT_DIM)
    lane = jnp.arange(LANES) % HEAD_DIM
    inv_row = inv[lane % (ROT_DIM // 2)].reshape(1, LANES)
    spec = pl.BlockSpec((ROPE_TM, LANES), lambda i: (i, 0))
    shp = jax.ShapeDtypeStruct((t, LANES), F32)
    return pl.pallas_call(
        _rope_kernel,
        grid=(t // ROPE_TM,),
        in_specs=[pl.BlockSpec((ROPE_TM, 1), lambda i: (i, 0)), pl.BlockSpec((1, LANES), lambda i: (0, 0))],
        out_specs=[spec, spec, spec],
        out_shape=[shp, shp, shp],
        compiler_params=_cparams(("parallel",)),
        name="rope_tables",
    )(pos, inv_row)


def _modulate_into(h_scr, x_ref, mod_ref):
    shift = mod_ref[0:1, :]
    scale = mod_ref[1:2, :]
    h_scr[...] = (x_ref[...] * (1.0 + scale) + shift).astype(BF16)


def _qkv_kernel(x_ref, mod_ref, w_ref, c_ref, s1_ref, s2_ref, o_ref, h_scr, r_scr, *, dilation):
    j = pl.program_id(1)

    @pl.when(j == 0)
    def _():
        _modulate_into(h_scr, x_ref, mod_ref)

    acc = jnp.dot(h_scr[...], w_ref[...], preferred_element_type=F32)
    n_col_blocks = acc.shape[1] // LANES

    @pl.when(j < 2)
    def _():
        c = c_ref[...]
        s1 = s1_ref[...]
        s2 = s2_ref[...]
        for cb in range(n_col_blocks):
            a = acc[:, cb * LANES:(cb + 1) * LANES]
            up = pltpu.roll(a, LANES - ROT_DIM // 2, 1)
            dn = pltpu.roll(a, ROT_DIM // 2, 1)
            r_scr[cb] = a * c + up * s1 + dn * s2

    @pl.when(j == 2)
    def _():
        for cb in range(n_col_blocks):
            r_scr[cb] = acc[:, cb * LANES:(cb + 1) * LANES]

    rows = r_scr.shape[1] // dilation
    for r in range(dilation):
        for cb in range(n_col_blocks):
            o_ref[r, :, cb * LANES:(cb + 1) * LANES] = (
                r_scr[cb, pl.ds(r, rows, stride=dilation), :].astype(BF16))


def _qkv_projection(x2, mod_l, w_bf, tabs, bsz, seq, dilation):
    t, d = x2.shape
    tiles_per_seq = seq // PROJ_TM
    tab_spec = pl.BlockSpec((PROJ_TM, LANES), lambda i, j: (i, 0))
    return pl.pallas_call(
        functools.partial(_qkv_kernel, dilation=dilation),
        grid=(t // PROJ_TM, 3),
        in_specs=[
            pl.BlockSpec((PROJ_TM, d), lambda i, j: (i, 0)),
            pl.BlockSpec((None, N_MODULATIONS, d), lambda i, j: (i // tiles_per_seq, 0, 0)),
            pl.BlockSpec((d, GROUP_WIDTH), lambda i, j: (0, j)),
            tab_spec, tab_spec, tab_spec,
        ],
        out_specs=pl.BlockSpec((None, dilation, PROJ_TM // dilation, GROUP_WIDTH),
                               lambda i, j: (i // tiles_per_seq, 0, i % tiles_per_seq, j)),
        out_shape=jax.ShapeDtypeStruct((bsz, dilation, seq // dilation, 3 * GROUP_WIDTH), BF16),
        scratch_shapes=[pltpu.VMEM((PROJ_TM, d), BF16),
                        pltpu.VMEM((GROUP_WIDTH // LANES, PROJ_TM, LANES), F32)],
        compiler_params=_cparams(("arbitrary", "arbitrary")),
        name=f"qkv_projection_d{dilation}",
    )(x2, mod_l, w_bf, *tabs)


def _sg_in_kernel(x_ref, mod_ref, w_ref, b_ref, o_ref, h_scr):
    @pl.when(pl.program_id(1) == 0)
    def _():
        _modulate_into(h_scr, x_ref, mod_ref)

    z = jnp.dot(h_scr[...], w_ref[...], preferred_element_type=F32) + b_ref[...]
    o_ref[...] = (0.5 * z * (1.0 + lax.erf(z * (2.0 ** -0.5)))).astype(BF16)


def _sg_in_projection(x2, mod_l, w_bf, b_in, seq):
    t, d = x2.shape
    n = w_bf.shape[1]
    tiles_per_seq = seq // PROJ_TM
    return pl.pallas_call(
        _sg_in_kernel,
        grid=(t // PROJ_TM, n // PROJ_TN),
        in_specs=[
            pl.BlockSpec((PROJ_TM, d), lambda i, j: (i, 0)),
            pl.BlockSpec((None, N_MODULATIONS, d), lambda i, j: (i // tiles_per_seq, 0, 0)),
            pl.BlockSpec((d, PROJ_TN), lambda i, j: (0, j)),
            pl.BlockSpec((1, PROJ_TN), lambda i, j: (0, j)),
        ],
        out_specs=pl.BlockSpec((PROJ_TM, PROJ_TN), lambda i, j: (i, j)),
        out_shape=jax.ShapeDtypeStruct((t, n), BF16),
        scratch_shapes=[pltpu.VMEM((PROJ_TM, d), BF16)],
        compiler_params=_cparams(("arbitrary", "arbitrary")),
        name="sg_in_projection",
    )(x2, mod_l, w_bf, b_in.reshape(1, n))


def _attn_kernel(q_ref, kp_ref, kc_ref, vp_ref, vc_ref, o_ref, lse_ref, kcat, vcat, s_scr, p_scr, m_scr):
    n = pl.program_id(2)
    blk = q_ref.shape[0]
    n_pairs = GROUP_WIDTH // LANES
    lane = lax.broadcasted_iota(jnp.int32, (1, LANES), 1)
    keep = [jnp.where(lane < HEAD_DIM, 1.0, 0.0).astype(BF16), jnp.where(lane < HEAD_DIM, 0.0, 1.0).astype(BF16)]

    for pair in range(n_pairs):
        cs = slice(pair * LANES, (pair + 1) * LANES)
        for half in range(2):
            for j, (k_ref, v_ref) in enumerate(((kp_ref, vp_ref), (kc_ref, vc_ref))):
                rs = slice((2 * half + j) * blk, (2 * half + j + 1) * blk)
                kcat[pair, rs, :] = k_ref[:, cs] * keep[half]
                vcat[pair, rs, :] = v_ref[:, cs] * keep[half]

    dn = (((1,), (1,)), ((), ()))
    for pair in range(n_pairs):
        cs = slice(pair * LANES, (pair + 1) * LANES)
        q = q_ref[:, cs] * (HEAD_DIM ** -0.5)
        s_scr[pair] = lax.dot_general(q, kcat[pair], dn, preferred_element_type=F32)

    row = lax.broadcasted_iota(jnp.int32, (blk, 4 * blk), 0)
    col = lax.broadcasted_iota(jnp.int32, (blk, 4 * blk), 1)
    key = col & (blk - 1)
    is_prev = (col & blk) == 0
    ok = (is_prev & (key >= row) & (n > 0)) | (jnp.logical_not(is_prev) & (key <= row))
    first_head = col < 2 * blk
    low_full = lax.broadcasted_iota(jnp.int32, (blk, LANES), 1) < HEAD_DIM
    for pair in range(n_pairs):
        s = jnp.where(ok, s_scr[pair], NEG_BIG)
        m_a = jnp.max(s[:, :2 * blk], axis=-1, keepdims=True)
        m_b = jnp.max(s[:, 2 * blk:], axis=-1, keepdims=True)
        p_scr[pair] = jnp.exp(s - jnp.where(first_head, m_a, m_b)).astype(BF16)
        m_scr[pair] = jnp.where(low_full, m_a, m_b)

    r = lax.broadcasted_iota(jnp.int32, (4 * blk, LANES), 0)
    l = lax.broadcasted_iota(jnp.int32, (4 * blk, LANES), 1)
    head_sum = jnp.where((r < 2 * blk) == (l < HEAD_DIM), 1.0, 0.0).astype(BF16)
    for pair in range(n_pairs):
        cs = slice(pair * LANES, (pair + 1) * LANES)
        p = p_scr[pair]
        o = jnp.dot(p, vcat[pair], preferred_element_type=F32)
        den = jnp.dot(p, head_sum, preferred_element_type=F32)
        o_ref[:, cs] = (o / den).astype(BF16)
        lse_ref[:, cs] = m_scr[pair] + jnp.log(den)


def _dilated_attention(qkv, dilation):
    bsz, _, length, _ = qkv.shape
    nblk = length // ATT_BLOCK
    n_pairs = GROUP_WIDTH // LANES

    def spec(part, prev):
        def index(b, r, n):
            return (b, r, jnp.maximum(n - 1, 0) if prev else n, part)
        return pl.BlockSpec((None, None, ATT_BLOCK, GROUP_WIDTH), index)

    out_spec = pl.BlockSpec((None, None, ATT_BLOCK, GROUP_WIDTH), lambda b, r, n: (b, r, n, 0))
    return pl.pallas_call(
        _attn_kernel,
        grid=(bsz, dilation, nblk),
        in_specs=[spec(0, False), spec(1, True), spec(1, False), spec(2, True), spec(2, False)],
        out_specs=[out_spec, out_spec],
        out_shape=[jax.ShapeDtypeStruct((bsz, dilation, length, GROUP_WIDTH), BF16),
                   jax.ShapeDtypeStruct((bsz, dilation, length, GROUP_WIDTH), F32)],
        scratch_shapes=[
            pltpu.VMEM((n_pairs, 4 * ATT_BLOCK, LANES), BF16),
            pltpu.VMEM((n_pairs, 4 * ATT_BLOCK, LANES), BF16),
            pltpu.VMEM((n_pairs, ATT_BLOCK, 4 * ATT_BLOCK), F32),
            pltpu.VMEM((n_pairs, ATT_BLOCK, 4 * ATT_BLOCK), BF16),
            pltpu.VMEM((n_pairs, ATT_BLOCK, LANES), F32),
        ],
        compiler_params=_cparams(("parallel", "parallel", "parallel")),
        name=f"dilated_attention_d{dilation}",
    )(qkv, qkv, qkv, qkv, qkv)


def _merge_kernel(*refs):
    ng = len(DILATED_GROUPS)
    o_refs, l_refs, out_ref = refs[:ng], refs[ng:2 * ng], refs[2 * ng]
    scratch = refs[2 * ng + 1:]
    outs, lses = [], []
    for g, (_, dilation) in enumerate(DILATED_GROUPS):
        if dilation == 1:
            outs.append(o_refs[g][0].astype(F32))
            lses.append(l_refs[g][0])
            continue
        so, sl = scratch[2 * (g - 1)], scratch[2 * (g - 1) + 1]
        n_col_blocks = so.shape[0]
        rows = so.shape[1] // dilation
        for r in range(dilation):
            for cb in range(n_col_blocks):
                cs = slice(cb * LANES, (cb + 1) * LANES)
                so[cb, pl.ds(r, rows, stride=dilation), :] = o_refs[g][r, :, cs].astype(F32)
                sl[cb, pl.ds(r, rows, stride=dilation), :] = l_refs[g][r, :, cs]
        outs.append(jnp.concatenate([so[cb] for cb in range(n_col_blocks)], axis=1))
        lses.append(jnp.concatenate([sl[cb] for cb in range(n_col_blocks)], axis=1))
    mx = jnp.maximum(jnp.maximum(lses[0], lses[1]), lses[2])
    es = [jnp.exp(l - mx) for l in lses]
    inv = 1.0 / (es[0] + es[1] + es[2])
    for g in range(ng):
        out_ref[:, g * GROUP_WIDTH:(g + 1) * GROUP_WIDTH] = (outs[g] * (es[g] * inv)).astype(BF16)


def _merge_groups(outs, lses, seq):
    bsz = outs[0].shape[0]
    tiles_per_seq = seq // MERGE_TM
    specs = [pl.BlockSpec((None, dilation, MERGE_TM // dilation, GROUP_WIDTH),
                          lambda i: (i // tiles_per_seq, 0, i % tiles_per_seq, 0))
             for _, dilation in DILATED_GROUPS]
    assert DILATED_GROUPS[0][1] == 1
    scratch = []
    for _ in DILATED_GROUPS[1:]:
        scratch += [pltpu.VMEM((GROUP_WIDTH // LANES, MERGE_TM, LANES), F32)] * 2
    return pl.pallas_call(
        _merge_kernel,
        grid=(bsz * tiles_per_seq,),
        in_specs=specs + specs,
        out_specs=pl.BlockSpec((MERGE_TM, ATT_WIDTH), lambda i: (i, 0)),
        out_shape=jax.ShapeDtypeStruct((bsz * seq, ATT_WIDTH), BF16),
        scratch_shapes=scratch,
        compiler_params=_cparams(("parallel",)),
        name="merge_groups",
    )(*outs, *lses)


def _sgu_kernel(u_ref, v_ref, g_ref, b_ref, wsp_ref, bsp_ref, o_ref):
    v = v_ref[...].astype(F32)
    vn = _layer_norm(v, g_ref[...], b_ref[...]).astype(BF16)
    row = lax.broadcasted_iota(jnp.int32, (SG_CHUNK, SG_CHUNK), 0)
    col = lax.broadcasted_iota(jnp.int32, (SG_CHUNK, SG_CHUNK), 1)
    causal = col <= row
    for g in range(SG_GROUPS):
        w = jnp.where(causal, wsp_ref[g], 0.0).astype(BF16)
        bias = bsp_ref[:, g:g + 1]
        cs = slice(g * SG_GROUP_DIM, (g + 1) * SG_GROUP_DIM)
        for c in range(u_ref.shape[0] // SG_CHUNK):
            rs = slice(c * SG_CHUNK, (c + 1) * SG_CHUNK)
            mixed = jnp.dot(w, vn[rs, cs], preferred_element_type=F32) + bias
            o_ref[rs, cs] = (u_ref[rs, cs].astype(F32) * mixed).astype(BF16)


def _spatial_gating(z, ln_g, ln_b, w_sp, b_sp):
    t = z.shape[0]
    return pl.pallas_call(
        _sgu_kernel,
        grid=(t // SGU_TM,),
        in_specs=[
            pl.BlockSpec((SGU_TM, SG_WIDTH), lambda i: (i, 0)),
            pl.BlockSpec((SGU_TM, SG_WIDTH), lambda i: (i, 1)),
            pl.BlockSpec((1, SG_WIDTH), lambda i: (0, 0)),
            pl.BlockSpec((1, SG_WIDTH), lambda i: (0, 0)),
            pl.BlockSpec((SG_GROUPS, SG_CHUNK, SG_CHUNK), lambda i: (0, 0, 0)),
            pl.BlockSpec((SG_CHUNK, SG_GROUPS), lambda i: (0, 0)),
        ],
        out_specs=pl.BlockSpec((SGU_TM, SG_WIDTH), lambda i: (i, 0)),
        out_shape=jax.ShapeDtypeStruct((t, SG_WIDTH), BF16),
        compiler_params=_cparams(("parallel",)),
        name="spatial_gating",
    )(z, z, ln_g.reshape(1, -1), ln_b.reshape(1, -1), w_sp, b_sp.T)


def _proj_ln_kernel(a_ref, w_ref, x_ref, mod_ref, g_ref, b_ref, rw_hi_ref, rw_lo_ref, rb_ref,
                    x_out, h_out, logit_out):
    y = jnp.dot(a_ref[...], w_ref[...], preferred_element_type=F32)
    gate = mod_ref[2:3, :]
    xn = _layer_norm(DEEPNORM_ALPHA * x_ref[...] + (1.0 + gate) * y, g_ref[...], b_ref[...])
    x_out[...] = xn
    h = xn * (1.0 + mod_ref[4:5, :]) + mod_ref[3:4, :]
    _store_token_major(h_out, h)
    h_hi = h.astype(BF16)
    h_lo = (h - h_hi.astype(F32)).astype(BF16)
    rw_hi = rw_hi_ref[...]
    logits = jnp.dot(h_hi, rw_hi, preferred_element_type=F32)
    logits = logits + jnp.dot(h_hi, rw_lo_ref[...], preferred_element_type=F32)
    logits = logits + jnp.dot(h_lo, rw_hi, preferred_element_type=F32)
    logit_out[...] = logits + rb_ref[...]


def _proj_ln(a, w_bf, x2, mod_l, ln_g, ln_b, router_w, router_b, seq):
    t, d = x2.shape
    k = a.shape[1]
    tiles_per_seq = seq // LN_TM
    rw_hi = router_w.astype(BF16)
    rw_lo = (router_w - rw_hi.astype(F32)).astype(BF16)
    row = lambda i: (i, 0)
    fixed = lambda i: (0, 0)
    return pl.pallas_call(
        _proj_ln_kernel,
        grid=(t // LN_TM,),
        in_specs=[
            pl.BlockSpec((LN_TM, k), row),
            pl.BlockSpec((k, d), fixed, pipeline_mode=pl.Buffered(1)),
            pl.BlockSpec((LN_TM, d), row),
            pl.BlockSpec((None, N_MODULATIONS, d), lambda i: (i // tiles_per_seq, 0, 0)),
            pl.BlockSpec((1, d), fixed),
            pl.BlockSpec((1, d), fixed),
            pl.BlockSpec((d, N_EXPERTS), fixed),
            pl.BlockSpec((d, N_EXPERTS), fixed),
            pl.BlockSpec((1, N_EXPERTS), fixed),
        ],
        out_specs=[pl.BlockSpec((LN_TM, d), row), pl.BlockSpec((LN_TM * TOKEN_CHUNKS, LANES), row),
                   pl.BlockSpec((LN_TM, N_EXPERTS), row)],
        out_shape=[jax.ShapeDtypeStruct((t, d), F32), jax.ShapeDtypeStruct((t * TOKEN_CHUNKS, LANES), F32),
                   jax.ShapeDtypeStruct((t, N_EXPERTS), F32)],
        compiler_params=_cparams(("parallel",)),
        name="proj_ln",
    )(a, w_bf, x2, mod_l, ln_g.reshape(1, d), ln_b.reshape(1, d), rw_hi, rw_lo,
      router_b.reshape(1, N_EXPERTS))


def _topk_kernel(l_ref, idx_ref, p_ref):
    l = l_ref[...]
    tm = l.shape[0]
    lane = lax.broadcasted_iota(jnp.int32, l.shape, 1)
    out_lane = lax.broadcasted_iota(jnp.int32, (tm, TOP_K), 1)
    idx_out = jnp.zeros((tm, TOP_K), jnp.int32)
    val_out = jnp.zeros((tm, TOP_K), F32)
    for k in range(TOP_K):
        m = jnp.max(l, axis=-1, keepdims=True)
        idx = jnp.min(jnp.where(l == m, lane, N_EXPERTS), axis=-1, keepdims=True)
        idx_out = jnp.where(out_lane == k, idx, idx_out)
        val_out = jnp.where(out_lane == k, m, val_out)
        l = jnp.where(lane == idx, -jnp.inf, l)
    e = jnp.exp(val_out - jnp.max(val_out, axis=-1, keepdims=True))
    idx_ref[...] = idx_out
    p_ref[...] = e / jnp.sum(e, axis=-1, keepdims=True)


def _route_topk(logits):
    t = logits.shape[0]
    return pl.pallas_call(
        _topk_kernel,
        grid=(t // TOPK_TM,),
        in_specs=[pl.BlockSpec((TOPK_TM, N_EXPERTS), lambda i: (i, 0))],
        out_specs=[pl.BlockSpec((TOPK_TM, TOP_K), lambda i: (i, 0))] * 2,
        out_shape=[jax.ShapeDtypeStruct((t, TOP_K), jnp.int32), jax.ShapeDtypeStruct((t, TOP_K), F32)],
        compiler_params=_cparams(("parallel",)),
        name="route_topk",
    )(logits)


def _slab_copy(src_hbm, dst_vmem, sem, src_tok, dst_tok):
    src = pl.multiple_of(src_tok * TOKEN_CHUNKS, TOKEN_CHUNKS)
    return pltpu.make_async_copy(src_hbm.at[pl.ds(src, TOKEN_CHUNKS)],
                                 dst_vmem.at[pl.ds(dst_tok * SLAB_PITCH, TOKEN_CHUNKS)], sem)


def _gather_kernel(tok_ref, h_hbm, o_ref, buf, sem):
    i = pl.program_id(0)
    n_steps = pl.num_programs(0)

    def start_all(step, slot):
        def body(r2, carry):
            for prio in range(2):
                r = 2 * r2 + prio
                _slab_copy(h_hbm, buf.at[slot], sem.at[slot], tok_ref[step * GATHER_ROWS + r], r).start(
                    priority=prio)
            return carry
        lax.fori_loop(0, GATHER_ROWS // 2, body, 0, unroll=4)

    def wait_all(slot):
        def body(r, carry):
            _slab_copy(h_hbm, buf.at[slot], sem.at[slot], 0, r).wait()
            return carry
        lax.fori_loop(0, GATHER_ROWS, body, 0, unroll=8)

    @pl.when(i == 0)
    def _():
        start_all(0, 0)

    for slot in range(2):
        @pl.when((i % 2 == slot) & (i + 1 < n_steps))
        def _():
            start_all(i + 1, 1 - slot)

        @pl.when(i % 2 == slot)
        def _():
            wait_all(slot)
            o_ref[...] = _load_token_major(buf.at[slot], GATHER_ROWS, SLAB_PITCH).astype(BF16)


def _moe_gather(h_tm, row_tok):
    n_rows = row_tok.shape[0]
    return pl.pallas_call(
        _gather_kernel,
        grid_spec=pltpu.PrefetchScalarGridSpec(
            num_scalar_prefetch=1,
            grid=(n_rows // GATHER_ROWS,),
            in_specs=[pl.BlockSpec(memory_space=pl.ANY)],
            out_specs=pl.BlockSpec((GATHER_ROWS, D_MODEL), lambda i, tok: (i, 0)),
            scratch_shapes=[pltpu.VMEM((2, GATHER_ROWS * SLAB_PITCH, LANES), F32),
                            pltpu.SemaphoreType.DMA((2,))],
        ),
        out_shape=jax.ShapeDtypeStruct((n_rows, D_MODEL), BF16),
        compiler_params=_cparams(("arbitrary",)),
        name="moe_gather",
    )(row_tok, h_tm)


def _experts_kernel(we_ref, ws_ref, wn_ref, nw_ref, tail_ref, xs_hbm, win_hbm, wout_hbm, bin_ref, bout_ref, y_hbm,
                    wbuf, wbf, xbuf, hid, ystash, ybuf, wsem, xsem, ysem, *, layer):
    w = pl.program_id(0)
    n_phase = FF_PHASES + OUT_PHASES
    assert n_phase % 2 == 0
    n_work = nw_ref[0]
    active = w < n_work
    sub_rows = MOE_TM

    k_half = wbuf.shape[2] // 2

    def weight_copy(src_hbm, expert, col_tile, sl, half, part):
        rows = pl.ds(part * k_half, k_half)
        return pltpu.make_async_copy(src_hbm.at[layer, expert, rows, pl.ds(col_tile * WEIGHT_TN, WEIGHT_TN)],
                                     wbuf.at[sl, half, rows], wsem.at[sl, half, part])

    def start_weights(item, phase):
        expert = we_ref[item]
        sl = phase % 2
        if phase < FF_PHASES:
            src, tiles = win_hbm, (phase, FF_PHASES + phase)
        else:
            src, tiles = wout_hbm, (2 * (phase - FF_PHASES), 2 * (phase - FF_PHASES) + 1)
        for half in range(2):
            for part in range(2):
                weight_copy(src, expert, tiles[half], sl, half, part).start(priority=part)

    def x_copy(item, piece, xsl):
        row = pl.multiple_of(ws_ref[item] + piece * MOE_PAD, MOE_PAD)
        return pltpu.make_async_copy(xs_hbm.at[pl.ds(row, MOE_PAD)],
                                     xbuf.at[xsl, pl.ds(piece * MOE_PAD, MOE_PAD)], xsem.at[xsl])

    def for_each_x_copy(item, xsl, fn):
        for piece in range(EXPERT_MAX_ROWS // MOE_PAD):
            @pl.when(piece < wn_ref[item])
            def _():
                fn(x_copy(item, piece, xsl))

    def y_copy(row, rows, ysl):
        dst = pl.multiple_of(row * TOKEN_CHUNKS, MOE_PAD * TOKEN_CHUNKS)
        return pltpu.make_async_copy(ybuf.at[ysl, pl.ds(0, rows * TOKEN_CHUNKS)],
                                     y_hbm.at[pl.ds(dst, rows * TOKEN_CHUNKS)], ysem.at[ysl])

    @pl.when(w == 0)
    def _():
        start_weights(0, 0)
        for_each_x_copy(0, 0, lambda cp: cp.start())
        ybuf[0] = jnp.zeros(ybuf.shape[1:], F32)

        def start_tail(i, carry):
            y_copy((tail_ref[0] + i) * MOE_PAD, MOE_PAD, 0).start()
            return carry

        def wait_tail(i, carry):
            y_copy(0, MOE_PAD, 0).wait()
            return carry

        lax.fori_loop(0, tail_ref[1], start_tail, 0)
        lax.fori_loop(0, tail_ref[1], wait_tail, 0)

    def for_each_tile(n_pieces, body):
        n_full = n_pieces // 2

        def full(sub, carry):
            body(pl.multiple_of(sub * sub_rows, sub_rows), sub_rows, sub)
            return carry

        lax.fori_loop(0, n_full, full, 0)

        @pl.when(n_pieces % 2 == 1)
        def _():
            body(pl.multiple_of(n_full * sub_rows, sub_rows), MOE_PAD, n_full)

    def ff_phase(f, n_pieces, xsl):
        bg = bin_ref[f:f + 1, :]
        bu = bin_ref[FF_PHASES + f:FF_PHASES + f + 1, :]

        def body(r0, rows, index):
            gu = jnp.dot(xbuf[xsl, pl.ds(r0, rows), :], wbf[...], preferred_element_type=F32)
            gate = jnp.minimum(gu[:, :WEIGHT_TN] + bg, SWIGLU_LIMIT)
            up = jnp.clip(gu[:, WEIGHT_TN:] + bu, -SWIGLU_LIMIT, SWIGLU_LIMIT)
            hid[f, pl.ds(r0, rows), :] = (
                (up + 1.0) * gate * jax.nn.sigmoid(SWIGLU_ALPHA * gate)).astype(BF16)

        for_each_tile(n_pieces, body)

    def out_half(r0, rows, bo):
        acc = bo
        for f in range(FF_PHASES):
            acc = acc + jnp.dot(hid[f, pl.ds(r0, rows), :], wbf[f * WEIGHT_TN:(f + 1) * WEIGHT_TN, :],
                                preferred_element_type=F32)
        return acc

    def out_phase_first(n_pieces):
        bo = bout_ref[0:1, :]

        def body(r0, rows, index):
            ystash[pl.ds(r0, rows), :] = out_half(r0, rows, bo)

        for_each_tile(n_pieces, body)

    def out_phase_second(n_pieces):
        bo = bout_ref[1:2, :]
        half_chunks = TOKEN_CHUNKS // 2

        def body(r0, rows, index):
            acc = out_half(r0, rows, bo)
            ysl = index % 2

            @pl.when(index >= 2)
            def _():
                y_copy(0, sub_rows, ysl).wait()

            stage = ybuf.at[ysl]
            for c in range(half_chunks):
                cs = slice(c * LANES, (c + 1) * LANES)
                stage[pl.ds(c, rows, stride=TOKEN_CHUNKS), :] = ystash[pl.ds(r0, rows), cs]
                stage[pl.ds(half_chunks + c, rows, stride=TOKEN_CHUNKS), :] = acc[:, cs]
            y_copy(ws_ref[w] + r0, rows, ysl).start()

        for_each_tile(n_pieces, body)
        n_full = n_pieces // 2
        has_rem = n_pieces % 2 == 1

        @pl.when(has_rem)
        def _():
            y_copy(0, MOE_PAD, n_full % 2).wait()

        @pl.when(n_full >= 1)
        def _():
            y_copy(0, sub_rows, (n_full - 1) % 2).wait()

        @pl.when(jnp.logical_not(has_rem) & (n_full >= 2))
        def _():
            y_copy(0, sub_rows, (n_full - 2) % 2).wait()

    @pl.when(active)
    def _():
        @pl.when(w + 1 < n_work)
        def _():
            for_each_x_copy(w + 1, (w + 1) % 2, lambda cp: cp.start())

        for_each_x_copy(w, w % 2, lambda cp: cp.wait())
        n_pieces = wn_ref[w]
        xsl = w % 2
        for p in range(n_phase):
            if p + 1 < n_phase:
                start_weights(w, p + 1)
            else:
                @pl.when(w + 1 < n_work)
                def _():
                    start_weights(w + 1, 0)

            for half in range(2):
                for part in range(2):
                    weight_copy(win_hbm, 0, 0, p % 2, half, part).wait()
                wbf[:, half * WEIGHT_TN:(half + 1) * WEIGHT_TN] = wbuf[p % 2, half].astype(BF16)
            if p < FF_PHASES:
                ff_phase(p, n_pieces, xsl)
            elif p == FF_PHASES:
                out_phase_first(n_pieces)
            else:
                out_phase_second(n_pieces)


def _expert_ffn(xs, work, w_in, b_in, w_out, b_out, layer):
    n_rows, d = xs.shape
    ne = w_in.shape[1]
    w_expert, w_start, w_nsub, n_work, tail = work
    n_items = w_expert.shape[0]
    assert OUT_PHASES == 2
    bias_in = pl.BlockSpec((None, 2 * FF_PHASES, WEIGHT_TN), lambda w, we, *_: (we[w], 0, 0))
    bias_out = pl.BlockSpec((None, OUT_PHASES, 2 * WEIGHT_TN), lambda w, we, *_: (we[w], 0, 0))
    any_spec = pl.BlockSpec(memory_space=pl.ANY)
    return pl.pallas_call(
        functools.partial(_experts_kernel, layer=layer),
        grid_spec=pltpu.PrefetchScalarGridSpec(
            num_scalar_prefetch=5,
            grid=(n_items,),
            in_specs=[any_spec, any_spec, any_spec, bias_in, bias_out],
            out_specs=any_spec,
            scratch_shapes=[
                pltpu.VMEM((2, 2, d, WEIGHT_TN), F32),
                pltpu.VMEM((d, 2 * WEIGHT_TN), BF16),
                pltpu.VMEM((2, EXPERT_MAX_ROWS, d), BF16),
                pltpu.VMEM((FF_PHASES, EXPERT_MAX_ROWS, WEIGHT_TN), BF16),
                pltpu.VMEM((EXPERT_MAX_ROWS, 2 * WEIGHT_TN), F32),
                pltpu.VMEM((2, MOE_TM * TOKEN_CHUNKS, LANES), F32),
                pltpu.SemaphoreType.DMA((2, 2, 2)),
                pltpu.SemaphoreType.DMA((2,)),
                pltpu.SemaphoreType.DMA((2,)),
            ],
        ),
        out_shape=jax.ShapeDtypeStruct((n_rows * TOKEN_CHUNKS, LANES), F32),
        compiler_params=_cparams(("arbitrary",)),
        name="expert_ffn",
    )(w_expert, w_start, w_nsub, n_work, tail, xs, w_in, w_out,
      b_in.reshape(ne, 2 * FF_PHASES, WEIGHT_TN), b_out.reshape(ne, OUT_PHASES, 2 * WEIGHT_TN))


def _combine_kernel(dest_ref, y_hbm, p_ref, x_ref, mod_ref, g_ref, b_ref, o_ref, buf, sem):
    i = pl.program_id(0)
    n_steps = pl.num_programs(0)

    def start_all(step, slot):
        def body(r, carry):
            for k in range(TOP_K):
                row = dest_ref[(step * COMBINE_TM + r) * TOP_K + k]
                _slab_copy(y_hbm, buf.at[slot, k], sem.at[slot], row, r).start(priority=k % 2)
            return carry
        lax.fori_loop(0, COMBINE_TM, body, 0, unroll=2)

    def wait_all(slot):
        def body(r, carry):
            for k in range(TOP_K):
                _slab_copy(y_hbm, buf.at[slot, k], sem.at[slot], 0, r).wait()
            return carry
        lax.fori_loop(0, COMBINE_TM, body, 0, unroll=2)

    @pl.when(i == 0)
    def _():
        start_all(0, 0)

    for slot in range(2):
        @pl.when((i % 2 == slot) & (i + 1 < n_steps))
        def _():
            start_all(i + 1, 1 - slot)

        @pl.when(i % 2 == slot)
        def _():
            wait_all(slot)
            p = p_ref[...]
            moe = p[:, 0:1] * _load_token_major(buf.at[slot, 0], COMBINE_TM, SLAB_PITCH)
            for k in range(1, TOP_K):
                moe = moe + p[:, k:k + 1] * _load_token_major(buf.at[slot, k], COMBINE_TM, SLAB_PITCH)
            gate = mod_ref[5:6, :]
            o_ref[...] = _layer_norm(DEEPNORM_ALPHA * x_ref[...] + (1.0 + gate) * moe, g_ref[...], b_ref[...])


def _combine_ln(y, dest, probs, x2, mod_l, ln_g, ln_b, seq):
    t, d = x2.shape
    tiles_per_seq = seq // COMBINE_TM
    return pl.pallas_call(
        _combine_kernel,
        grid_spec=pltpu.PrefetchScalarGridSpec(
            num_scalar_prefetch=1,
            grid=(t // COMBINE_TM,),
            in_specs=[
                pl.BlockSpec(memory_space=pl.ANY),
                pl.BlockSpec((COMBINE_TM, TOP_K), lambda i, dst: (i, 0)),
                pl.BlockSpec((COMBINE_TM, d), lambda i, dst: (i, 0)),
                pl.BlockSpec((None, N_MODULATIONS, d), lambda i, dst: (i // tiles_per_seq, 0, 0)),
                pl.BlockSpec((1, d), lambda i, dst: (0, 0)),
                pl.BlockSpec((1, d), lambda i, dst: (0, 0)),
            ],
            out_specs=pl.BlockSpec((COMBINE_TM, d), lambda i, dst: (i, 0)),
            scratch_shapes=[pltpu.VMEM((2, TOP_K, COMBINE_TM * SLAB_PITCH, LANES), F32),
                            pltpu.SemaphoreType.DMA((2,))],
        ),
        out_shape=jax.ShapeDtypeStruct((t, d), F32),
        compiler_params=_cparams(("arbitrary",)),
        name="combine_ln",
    )(dest, y, probs, x2, mod_l, ln_g.reshape(1, d), ln_b.reshape(1, d))


def _routing_tables(top_idx):
    t = top_idx.shape[0]
    flat_e = top_idx.reshape(-1)
    flat_t = jnp.repeat(jnp.arange(t, dtype=jnp.int32), TOP_K)
    onehot = jax.nn.one_hot(flat_e, N_EXPERTS, dtype=jnp.int32)
    counts = jnp.sum(onehot, axis=0)
    rank = jnp.take_along_axis(jnp.cumsum(onehot, axis=0) - onehot, flat_e[:, None], axis=1)[:, 0]
    padded = (counts + MOE_PAD - 1) // MOE_PAD * MOE_PAD
    ends = jnp.cumsum(padded)
    starts = ends - padded
    dest = (starts[flat_e] + rank).astype(jnp.int32)
    n_rows = t * TOP_K + N_EXPERTS * MOE_PAD
    row_tok = (jnp.arange(n_rows, dtype=jnp.int32) % t).at[dest].set(flat_t)
    chunks = (padded + EXPERT_MAX_ROWS - 1) // EXPERT_MAX_ROWS
    chunk_ends = jnp.cumsum(chunks)
    n_work = chunk_ends[-1:]
    item = jnp.arange(N_EXPERTS + n_rows // EXPERT_MAX_ROWS, dtype=jnp.int32)
    w_expert = jnp.minimum(jnp.sum(item[:, None] >= chunk_ends[None, :], axis=1), N_EXPERTS - 1)
    w_chunk = item - (chunk_ends[w_expert] - chunks[w_expert])
    live = item < n_work
    w_start = jnp.where(live, starts[w_expert] + w_chunk * EXPERT_MAX_ROWS, 0)
    w_rows = jnp.clip(padded[w_expert] - w_chunk * EXPERT_MAX_ROWS, 0, EXPERT_MAX_ROWS)
    w_nsub = jnp.where(live, w_rows // MOE_PAD, 0)
    used_tiles = ends[-1] // MOE_PAD
    tail = jnp.stack([used_tiles, n_rows // MOE_PAD - used_tiles])
    work = tuple(a.astype(jnp.int32) for a in (w_expert, w_start, w_nsub, n_work, tail))
    return dest, row_tok, work


def _moe_block(h, logits, x2, mod_l, ln_g, ln_b, w_in, b_in, w_out, b_out, layer, seq):
    top_idx, probs = _route_topk(logits)
    dest, row_tok, work = _routing_tables(top_idx)
    xs = _moe_gather(h, row_tok)
    y = _expert_ffn(xs, work, w_in, b_in, w_out, b_out, layer)
    return _combine_ln(y, dest, probs, x2, mod_l, ln_g, ln_b, seq)


def kernel(x, c, positions, cond_w, cond_b, ln_g, ln_b, attn_w_qkv, attn_w_o, sg_w_in, sg_b_in, sg_ln_g, sg_ln_b, sg_w_spatial, sg_b_spatial, sg_w_out, router_w, router_b, expert_w_in, expert_b_in, expert_w_out, expert_b_out):
    bsz, seq, d = x.shape
    x2 = x.reshape(bsz * seq, d)
    mod = _modulation(c, cond_w, cond_b).reshape(DEPTH, bsz, N_MODULATIONS, d)
    tabs = _rope_tables(positions)

    w_qkv = attn_w_qkv[0].astype(BF16).reshape(d, 3, len(DILATED_GROUPS), GROUP_WIDTH)
    outs, lses = [], []
    for g, (_, dilation) in enumerate(DILATED_GROUPS):
        w_g = w_qkv[:, :, g, :].reshape(d, 3 * GROUP_WIDTH)
        qkv = _qkv_projection(x2, mod[0], w_g, tabs, bsz, seq, dilation)
        o, lse = _dilated_attention(qkv, dilation)
        outs.append(o)
        lses.append(lse)
    mixed = _merge_groups(outs, lses, seq)
    x2, h, logits = _proj_ln(mixed, attn_w_o[0].astype(BF16), x2, mod[0], ln_g[0, 0], ln_b[0, 0],
                             router_w[0], router_b[0], seq)
    x2 = _moe_block(h, logits, x2, mod[0], ln_g[0, 1], ln_b[0, 1], expert_w_in, expert_b_in[0],
                    expert_w_out, expert_b_out[0], 0, seq)

    z = _sg_in_projection(x2, mod[1], sg_w_in[0].astype(BF16), sg_b_in[0], seq)
    gated = _spatial_gating(z, sg_ln_g[0], sg_ln_b[0], sg_w_spatial[0], sg_b_spatial[0])
    x2, h, logits = _proj_ln(gated, sg_w_out[0].astype(BF16), x2, mod[1], ln_g[1, 0], ln_b[1, 0],
                             router_w[1], router_b[1], seq)
    x2 = _moe_block(h, logits, x2, mod[1], ln_g[1, 1], ln_b[1, 1], expert_w_in, expert_b_in[1],
                    expert_w_out, expert_b_out[1], 1, seq)
    return x2.reshape(bsz, seq, d)
```
